```python
import math
import jax, jax.numpy as jnp
from jax import lax
import numpy as np

D_MODEL = 2048
BATCH = 4
SEQ = 4096
DEPTH = 4

N_MIXERS = 2
N_A_LAYERS = (DEPTH + 1) // 2
N_B_LAYERS = DEPTH // 2
DA_HEAD_DIM = 128
DA_HEADS = D_MODEL // (2 * DA_HEAD_DIM)
SW_HEAD_DIM = 64
SW_HEADS = D_MODEL // SW_HEAD_DIM
SW_KV_HEADS = SW_HEADS // 8
SW_GROUP = SW_HEADS // SW_KV_HEADS
SW_QKV_WIDTH = (SW_HEADS + 2 * SW_KV_HEADS) * SW_HEAD_DIM
WINDOW = 128
Q_BLOCK = 128
D_FF = 4 * D_MODEL
ROPE_THETA = 10000.0
LN_EPS = 1e-5
ALPHA = (2 * DEPTH) ** 0.25
BETA = (8 * DEPTH) ** -0.25
N_MOD = 6

kernel_name = "hybrid_diffattn_swa_sink_deepnorm_adaln"


def rope_tables(seq, dim):
    inv = 1.0 / (ROPE_THETA ** (jnp.arange(0, dim, 2, dtype=jnp.float32) / dim))
    ang = jnp.arange(seq, dtype=jnp.float32)[:, None] * inv[None, :]
    return jnp.cos(ang), jnp.sin(ang)


def apply_rope(x, cos, sin):
    bshape = (x.shape[1],) + (1,) * (x.ndim - 3) + (cos.shape[-1],)
    cs = cos.reshape(bshape).astype(x.dtype)
    sn = sin.reshape(bshape).astype(x.dtype)
    x1, x2 = jnp.split(x, 2, axis=-1)
    return jnp.concatenate([x1 * cs - x2 * sn, x2 * cs + x1 * sn], axis=-1)


def layer_norm(x, g, b):
    xf = x.astype(jnp.float32)
    mu = jnp.mean(xf, axis=-1, keepdims=True)
    var = jnp.mean(jnp.square(xf - mu), axis=-1, keepdims=True)
    y = (xf - mu) * lax.rsqrt(var + LN_EPS)
    return (y * g.astype(jnp.float32) + b.astype(jnp.float32)).astype(x.dtype)


def diff_attention(h, w_qkv, w_o, lq1, lk1, lq2, lk2, subln_g, layer_idx, cos, sin):
    B, S, _ = h.shape
    H, d = DA_HEADS, DA_HEAD_DIM
    q, k, v = jnp.split(h @ w_qkv, 3, axis=-1)
    q = apply_rope(q.reshape(B, S, 2 * H, d), cos, sin) * (d ** -0.5)
    k = apply_rope(k.reshape(B, S, 2 * H, d), cos, sin)
    v = v.reshape(B, S, H, 2 * d)
    lam_init = 0.8 - 0.6 * math.exp(-0.3 * layer_idx)
    f32 = jnp.float32
    lam = (jnp.exp(jnp.sum(lq1.astype(f32) * lk1.astype(f32)))
           - jnp.exp(jnp.sum(lq2.astype(f32) * lk2.astype(f32))) + lam_init)
    nb = S // Q_BLOCK
    q_blocks = q.reshape(B, nb, Q_BLOCK, 2 * H, d).transpose(1, 0, 2, 3, 4)
    kpos = jnp.arange(S)

    def block(args):
        qb, n = args
        s = jnp.einsum('bqhd,bkhd->bhqk', qb, k).astype(f32)
        qpos = n * Q_BLOCK + jnp.arange(Q_BLOCK)
        s = jnp.where(kpos[None, :] <= qpos[:, None], s, -jnp.inf)
        p = jax.nn.softmax(s, axis=-1).reshape(B, H, 2, Q_BLOCK, S)
        a = (p[:, :, 0] - lam * p[:, :, 1]).astype(v.dtype)
        return jnp.einsum('bhqk,bkhe->bqhe', a, v)

    o = lax.map(block, (q_blocks, jnp.arange(nb)))
    o = o.transpose(1, 0, 2, 3, 4).reshape(B, S, H, 2 * d)
    of = o.astype(f32)
    of = of * lax.rsqrt(jnp.mean(of * of, axis=-1, keepdims=True) + LN_EPS)
    o = (of * subln_g.astype(f32) * (1.0 - lam_init)).astype(h.dtype)
    return o.reshape(B, S, H * 2 * d) @ w_o


def sliding_window_attention(h, w_qkv, b_qkv, sinks, w_o, b_o, cos, sin):
    B, S, _ = h.shape
    H, KV, G, d = SW_HEADS, SW_KV_HEADS, SW_GROUP, SW_HEAD_DIM
    qkv = h @ w_qkv + b_qkv
    q = qkv[..., :H * d].reshape(B, S, KV, G, d)
    k = qkv[..., H * d:(H + KV) * d].reshape(B, S, KV, d)
    v = qkv[..., (H + KV) * d:].reshape(B, S, KV, d)
    q = apply_rope(q, cos, sin) * (d ** -0.5)
    k = apply_rope(k, cos, sin)
    nb = S // Q_BLOCK
    qb = q.reshape(B, nb, Q_BLOCK, KV, G, d)
    kb = k.reshape(B, nb, Q_BLOCK, KV, d)
    vb = v.reshape(B, nb, Q_BLOCK, KV, d)
    pad = ((0, 0), (1, 0), (0, 0), (0, 0), (0, 0))
    kw = jnp.concatenate([jnp.pad(kb, pad)[:, :-1], kb], axis=2)
    vw = jnp.concatenate([jnp.pad(vb, pad)[:, :-1], vb], axis=2)
    f32 = jnp.float32
    s = jnp.einsum('bnqkgd,bnjkd->bnkgqj', qb, kw).astype(f32)
    blk = jnp.arange(nb)[:, None]
    qpos = blk * Q_BLOCK + jnp.arange(Q_BLOCK)[None, :]
    kpos = blk * Q_BLOCK - Q_BLOCK + jnp.arange(2 * Q_BLOCK)[None, :]
    diff = qpos[:, :, None] - kpos[:, None, :]
    valid = (diff >= 0) & (diff < WINDOW) & (kpos[:, None, :] >= 0)
    s = jnp.where(valid[None, :, None, None], s, -jnp.inf)
    sink = sinks.astype(f32).reshape(1, 1, KV, G, 1, 1)
    m = jnp.maximum(jnp.max(s, axis=-1, keepdims=True), sink)
    e = jnp.exp(s - m)
    p = e / (jnp.sum(e, axis=-1, keepdims=True) + jnp.exp(sink - m))
    o = jnp.einsum('bnkgqj,bnjkd->bnqkgd', p.astype(v.dtype), vw)
    return o.reshape(B, S, H * d) @ w_o + b_o


def squared_relu_mlp(h, w_up, w_down):
    return jnp.square(jax.nn.relu(h @ w_up)) @ w_down


def setup_inputs(seed: int = 0) -> dict:
    key = jax.random.key(seed)
    ks = jax.random.split(key, 24)
    f32 = jnp.float32

    def nrm(k, shape, s):
        return jax.random.normal(k, shape, f32) * s

    D = D_MODEL
    return {
        "x": nrm(ks[0], (BATCH, SEQ, D), 1.0),
        "c": nrm(ks[1], (BATCH, D), 1.0),
        "w_ada": nrm(ks[2], (DEPTH, D, N_MOD * D), 0.2 * D ** -0.5),
        "b_ada": nrm(ks[3], (DEPTH, N_MOD * D), 0.02),
        "ln_mix_g": 1.0 + nrm(ks[4], (DEPTH, D), 0.02),
        "ln_mix_b": nrm(ks[5], (DEPTH, D), 0.02),
        "ln_mlp_g": 1.0 + nrm(ks[6], (DEPTH, D), 0.02),
        "ln_mlp_b": nrm(ks[7], (DEPTH, D), 0.02),
        "w_up": nrm(ks[8], (DEPTH, D, D_FF), D ** -0.5),
        "w_down": nrm(ks[9], (DEPTH, D_FF, D), BETA * D_FF ** -0.5),
        "a_w_qkv": nrm(ks[10], (N_A_LAYERS, D, 3 * D), D ** -0.5),
        "a_w_o": nrm(ks[11], (N_A_LAYERS, D, D), BETA * D ** -0.5),
        "a_lambda_q1": nrm(ks[12], (N_A_LAYERS, DA_HEAD_DIM), 0.1),
        "a_lambda_k1": nrm(ks[13], (N_A_LAYERS, DA_HEAD_DIM), 0.1),
        "a_lambda_q2": nrm(ks[14], (N_A_LAYERS, DA_HEAD_DIM), 0.1),
        "a_lambda_k2": nrm(ks[15], (N_A_LAYERS, DA_HEAD_DIM), 0.1),
        "a_subln_g": 1.0 + nrm(ks[16], (N_A_LAYERS, 2 * DA_HEAD_DIM), 0.02),
        "b_w_qkv": nrm(ks[17], (N_B_LAYERS, D, SW_QKV_WIDTH), D ** -0.5),
        "b_b_qkv": nrm(ks[18], (N_B_LAYERS, SW_QKV_WIDTH), 0.02),
        "b_sinks": nrm(ks[19], (N_B_LAYERS, SW_HEADS), 0.5),
        "b_w_o": nrm(ks[20], (N_B_LAYERS, D, D), BETA * D ** -0.5),
        "b_b_o": nrm(ks[21], (N_B_LAYERS, D), 0.02),
    }


def reference(x, c, w_ada, b_ada, ln_mix_g, ln_mix_b, ln_mlp_g, ln_mlp_b, w_up, w_down,
              a_w_qkv, a_w_o, a_lambda_q1, a_lambda_k1, a_lambda_q2, a_lambda_k2, a_subln_g,
              b_w_qkv, b_b_qkv, b_sinks, b_w_o, b_b_o):
    S = x.shape[1]
    cos_a, sin_a = rope_tables(S, DA_HEAD_DIM)
    cos_b, sin_b = rope_tables(S, SW_HEAD_DIM)
    c_act = jax.nn.silu(c)
    for i in range(DEPTH):
        mod = (c_act @ w_ada[i] + b_ada[i])[:, None, :]
        sh_a, sc_a, g_a, sh_m, sc_m, g_m = jnp.split(mod, N_MOD, axis=-1)
        h = x * (1.0 + sc_a) + sh_a
        if i % N_MIXERS == 0:
            j = i // N_MIXERS
            y = diff_attention(h, a_w_qkv[j], a_w_o[j], a_lambda_q1[j], a_lambda_k1[j],
                               a_lambda_q2[j], a_lambda_k2[j], a_subln_g[j], i, cos_a, sin_a)
        else:
            j = i // N_MIXERS
            y = sliding_window_attention(h, b_w_qkv[j], b_b_qkv[j], b_sinks[j], b_w_o[j],
                                         b_b_o[j], cos_b, sin_b)
        x = layer_norm(ALPHA * x + (1.0 + g_a) * y, ln_mix_g[i], ln_mix_b[i])
        h = x * (1.0 + sc_m) + sh_m
        y = squared_relu_mlp(h, w_up[i], w_down[i])
        x = layer_norm(ALPHA * x + (1.0 + g_m) * y, ln_mlp_g[i], ln_mlp_b[i])
    return x
```

```python
import functools
import math

import jax
import jax.numpy as jnp
from jax import lax
from jax.experimental import pallas as pl
from jax.experimental.pallas import tpu as pltpu

F32 = jnp.float32
BF16 = jnp.bfloat16

LANES = 128
DA_HEAD_DIM = 128
SW_HEAD_DIM = 64
SW_GROUP = 8
WINDOW = 128
N_MOD = 6
ROPE_THETA = 10000.0
LN_EPS = 1e-5
VMEM_LIMIT_BYTES = 56 * 1024 * 1024

SH_A, SC_A, G_A, SH_M, SC_M, G_M = range(N_MOD)

_NT = (((1,), (1,)), ((), ()))


def _params(*sem):
    return pltpu.CompilerParams(dimension_semantics=sem, vmem_limit_bytes=VMEM_LIMIT_BYTES)


def _ada_kernel(c_ref, w_ref, b_ref, o_ref):
    c = c_ref[...]
    ca = (c * (1.0 / (1.0 + jnp.exp(-c)))).astype(BF16)
    y = jnp.dot(ca, w_ref[...].astype(BF16), preferred_element_type=F32)
    o_ref[...] = y + b_ref[...]


def _ada_mod(c, w_ada, b_ada):
    depth, d, n = w_ada.shape
    b = c.shape[0]
    rows = 8
    bn = min(n, 1024)
    c8 = jnp.zeros((rows, d), F32).at[:b].set(c)
    out = pl.pallas_call(
        _ada_kernel,
        grid=(depth, n // bn),
        in_specs=[
            pl.BlockSpec((rows, d), lambda l, j: (0, 0)),
            pl.BlockSpec((None, d, bn), lambda l, j: (l, 0, j)),
            pl.BlockSpec((None, 1, bn), lambda l, j: (l, 0, j)),
        ],
        out_specs=pl.BlockSpec((None, rows, bn), lambda l, j: (l, 0, j)),
        out_shape=jax.ShapeDtypeStruct((depth, rows, n), F32),
        compiler_params=_params("parallel", "parallel"),
        name="ada_mod",
    )(c8, w_ada, b_ada.reshape(depth, 1, n))
    return out[:, :b].reshape(depth, b, 1, n)


def _rope_tables(seq, dim, scale):
    inv = 1.0 / (ROPE_THETA ** (jnp.arange(0, dim, 2, dtype=F32) / dim))
    ang = jnp.arange(seq, dtype=F32)[:, None] * inv[None, :]
    cos, sin = jnp.cos(ang), jnp.sin(ang)
    reps = LANES // dim
    cos_t = jnp.tile(jnp.concatenate([cos, cos], -1), (1, reps))
    sin_t = jnp.tile(jnp.concatenate([-sin, sin], -1), (1, reps))
    return jnp.stack([cos_t * scale, sin_t * scale, cos_t, sin_t])


def _rope(x, cos, sin, half):
    if 2 * half == LANES:
        partner = pltpu.roll(x, half, axis=1)
    else:
        lane = lax.broadcasted_iota(jnp.int32, x.shape, 1)
        lower = (lane % (2 * half)) < half
        partner = jnp.where(lower, pltpu.roll(x, LANES - half, axis=1), pltpu.roll(x, half, axis=1))
    return x * cos + partner * sin


def _modulate(x_ref, sc_ref, sh_ref):
    return (x_ref[...] * (1.0 + sc_ref[...]) + sh_ref[...]).astype(BF16)


def _qkv_diff_kernel(x_ref, sc_ref, sh_ref, w_ref, tab_ref, o_ref):
    j = pl.program_id(0)
    y = jnp.dot(_modulate(x_ref, sc_ref, sh_ref), w_ref[...], preferred_element_type=F32)

    @pl.when(j == 2)
    def _():
        o_ref[...] = y.astype(BF16)

    @pl.when(j < 2)
    def _():
        cos, sin = tab_ref[0], tab_ref[1]
        for g in range(y.shape[1] // LANES):
            cols = slice(g * LANES, (g + 1) * LANES)
            o_ref[:, cols] = _rope(y[:, cols], cos, sin, DA_HEAD_DIM // 2).astype(BF16)


def _qkv_diff(x2, mod_l, w_bf, tabs, seq, bm):
    m, d = x2.shape
    tpb = seq // bm
    return pl.pallas_call(
        _qkv_diff_kernel,
        grid=(3, m // bm),
        in_specs=[
            pl.BlockSpec((bm, d), lambda j, i: (i, 0)),
            pl.BlockSpec((None, 1, d), lambda j, i: (i // tpb, 0, SC_A)),
            pl.BlockSpec((None, 1, d), lambda j, i: (i // tpb, 0, SH_A)),
            pl.BlockSpec((d, d), lambda j, i: (0, j)),
            pl.BlockSpec((2, bm, LANES), lambda j, i: (jnp.minimum(j, 1), i % tpb, 0)),
        ],
        out_specs=pl.BlockSpec((bm, d), lambda j, i: (i, j)),
        out_shape=jax.ShapeDtypeStruct((m, 3 * d), BF16),
        compiler_params=_params("parallel", "parallel"),
        name="qkv_diff",
    )(x2, mod_l, mod_l, w_bf, tabs.reshape(4, seq, LANES))


def _diff_attn_kernel(q_ref, k_ref, v_ref, lq1_ref, lk1_ref, lq2_ref, lk2_ref, g_ref, o_ref,
                      acc_ref, m_ref, l_ref, *, blk, lam_init):
    qi = pl.program_id(2)
    d = DA_HEAD_DIM
    q = q_ref[...]
    q1, q2 = q[:, :d], q[:, d:]
    m_ref[...] = jnp.full(m_ref.shape, -jnp.inf, F32)
    l_ref[...] = jnp.zeros(l_ref.shape, F32)
    acc_ref[...] = jnp.zeros(acc_ref.shape, F32)

    def step(j, masked):
        rows = pl.ds(pl.multiple_of(j * blk, blk), blk)
        k = k_ref[rows, :]
        v = v_ref[rows, :]
        s1 = lax.dot_general(q1, k[:, :d], _NT, preferred_element_type=F32)
        s2 = lax.dot_general(q2, k[:, d:], _NT, preferred_element_type=F32)
        s = jnp.concatenate([s1, s2], axis=0)
        if masked:
            r = lax.broadcasted_iota(jnp.int32, s.shape, 0)
            c = lax.broadcasted_iota(jnp.int32, s.shape, 1)
            s = jnp.where(c <= r % blk, s, -jnp.inf)
        m_prev = m_ref[...]
        m_new = jnp.maximum(m_prev, jnp.max(s, axis=-1, keepdims=True))
        alpha = jnp.exp(m_prev - m_new)
        p = jnp.exp(s - m_new)
        l_ref[...] = alpha * l_ref[...] + jnp.sum(p, axis=-1, keepdims=True)
        acc_ref[...] = alpha * acc_ref[...] + jnp.dot(p.astype(BF16), v, preferred_element_type=F32)
        m_ref[...] = m_new

    def body(j, carry):
        step(j, False)
        return carry

    lax.fori_loop(0, qi, body, 0)
    step(qi, True)

    lam = (jnp.exp(jnp.sum(lq1_ref[...] * lk1_ref[...], keepdims=True))
           - jnp.exp(jnp.sum(lq2_ref[...] * lk2_ref[...], keepdims=True)) + lam_init)
    o = acc_ref[...] / l_ref[...]
    of = o[:blk] - lam * o[blk:]
    of = of * lax.rsqrt(jnp.mean(of * of, axis=-1, keepdims=True) + LN_EPS)
    o_ref[...] = (of * g_ref[...] * (1.0 - lam_init)).astype(BF16)


def _diff_attn(qkv, lam_vecs, subln_g, layer_idx, batch, seq, blk):
    m, d3 = qkv.shape
    d = d3 // 3
    w = 2 * DA_HEAD_DIM
    heads = d // w
    nq = seq // blk
    lam_init = 0.8 - 0.6 * math.exp(-0.3 * layer_idx)
    vec = pl.BlockSpec((1, DA_HEAD_DIM), lambda b, h, i: (0, 0))
    return pl.pallas_call(
        functools.partial(_diff_attn_kernel, blk=blk, lam_init=lam_init),
        grid=(batch, heads, nq),
        in_specs=[
            pl.BlockSpec((blk, w), lambda b, h, i: (b * nq + i, h)),
            pl.BlockSpec((seq, w), lambda b, h, i: (b, heads + h)),
            pl.BlockSpec((seq, w), lambda b, h, i: (b, 2 * heads + h)),
            vec, vec, vec, vec,
            pl.BlockSpec((1, w), lambda b, h, i: (0, 0)),
        ],
        out_specs=pl.BlockSpec((blk, w), lambda b, h, i: (b * nq + i, h)),
        out_shape=jax.ShapeDtypeStruct((m, d), BF16),
        scratch_shapes=[
            pltpu.VMEM((2 * blk, w), F32),
            pltpu.VMEM((2 * blk, 1), F32),
            pltpu.VMEM((2 * blk, 1), F32),
        ],
        compiler_params=_params("parallel", "parallel", "arbitrary"),
        name="diff_attn",
    )(qkv, qkv, qkv, *[v.reshape(1, DA_HEAD_DIM) for v in lam_vecs], subln_g.reshape(1, w))


def _qkv_swa_kernel(x_ref, sc_ref, sh_ref, w_ref, b_ref, tab_ref, q_ref, k_ref, v_ref):
    y = jnp.dot(_modulate(x_ref, sc_ref, sh_ref), w_ref[...], preferred_element_type=F32) + b_ref[...]
    half = SW_HEAD_DIM // 2
    dq = q_ref.shape[1]
    dkv = k_ref.shape[1] // 2
    cq, sq, ck, sk = tab_ref[0], tab_ref[1], tab_ref[2], tab_ref[3]
    for g in range(dq // LANES):
        cols = slice(g * LANES, (g + 1) * LANES)
        q_ref[:, cols] = _rope(y[:, cols], cq, sq, half).astype(BF16)
    lane = lax.broadcasted_iota(jnp.int32, (x_ref.shape[0], LANES), 1)
    lower = lane < SW_HEAD_DIM

    def spread(t, ref, p):
        swapped = pltpu.roll(t, SW_HEAD_DIM, axis=1)
        ref[:, (2 * p) * LANES:(2 * p + 1) * LANES] = jnp.where(lower, t, swapped).astype(BF16)
        ref[:, (2 * p + 1) * LANES:(2 * p + 2) * LANES] = jnp.where(lower, swapped, t).astype(BF16)

    for p in range(dkv // LANES):
        kc = slice(dq + p * LANES, dq + (p + 1) * LANES)
        vc = slice(dq + dkv + p * LANES, dq + dkv + (p + 1) * LANES)
        spread(_rope(y[:, kc], ck, sk, half), k_ref, p)
        spread(y[:, vc], v_ref, p)


def _qkv_swa(x2, mod_l, w_bf, bias, tabs, seq, bm):
    m, d = x2.shape
    n = w_bf.shape[1]
    dkv = (n - d) // 2
    assert dkv % LANES == 0
    tpb = seq // bm
    kv_shape = jax.ShapeDtypeStruct((m, 2 * dkv), BF16)
    return pl.pallas_call(
        _qkv_swa_kernel,
        grid=(m // bm,),
        in_specs=[
            pl.BlockSpec((bm, d), lambda i: (i, 0)),
            pl.BlockSpec((None, 1, d), lambda i: (i // tpb, 0, SC_A)),
            pl.BlockSpec((None, 1, d), lambda i: (i // tpb, 0, SH_A)),
            pl.BlockSpec((d, n), lambda i: (0, 0)),
            pl.BlockSpec((1, n), lambda i: (0, 0)),
            pl.BlockSpec((4, bm, LANES), lambda i: (0, i % tpb, 0)),
        ],
        out_specs=[
            pl.BlockSpec((bm, d), lambda i: (i, 0)),
            pl.BlockSpec((bm, 2 * dkv), lambda i: (i, 0)),
            pl.BlockSpec((bm, 2 * dkv), lambda i: (i, 0)),
        ],
        out_shape=[jax.ShapeDtypeStruct((m, d), BF16), kv_shape, kv_shape],
        compiler_params=_params("parallel"),
        name="qkv_swa",
    )(x2, mod_l, mod_l, w_bf, bias.reshape(1, n), tabs)


def _swa_kernel(sink_ref, q_ref, kc_ref, kp_ref, vc_ref, vp_ref, o_ref, *, tq):
    n = pl.program_id(1)
    blk = WINDOW
    pairs = SW_GROUP // 2
    kvh_count = kc_ref.shape[1] // LANES
    rows8 = SW_GROUP * blk

    r = lax.broadcasted_iota(jnp.int32, (rows8, 2 * blk), 0) % blk
    c = lax.broadcasted_iota(jnp.int32, (rows8, 2 * blk), 1)
    delta = c - r - 1
    neg = jnp.float32(-jnp.inf)
    bias_rest = jnp.where((delta >= 0) & (delta < blk), 0.0, neg)
    lo_bound = jnp.where(n == 0, blk, 0)
    bias_first = jnp.where(c >= lo_bound, bias_rest, neg)

    lane = lax.broadcasted_iota(jnp.int32, (1, LANES), 1)
    m_lo = jnp.where(lane < SW_HEAD_DIM, 1.0, 0.0).astype(BF16)
    m_hi = jnp.where(lane < SW_HEAD_DIM, 0.0, 1.0).astype(BF16)
    lower = lax.broadcasted_iota(jnp.int32, (blk, LANES), 1) < SW_HEAD_DIM

    for i in range(tq // blk):
        qrows = slice(i * blk, (i + 1) * blk)
        for kvh in range(kvh_count):
            kcols = slice(kvh * LANES, (kvh + 1) * LANES)
            if i == 0:
                kwin = jnp.concatenate([kp_ref[:, kcols], kc_ref[0:blk, kcols]], axis=0)
                vwin = jnp.concatenate([vp_ref[:, kcols], vc_ref[0:blk, kcols]], axis=0)
                bias = bias_first
            else:
                kwin = kc_ref[(i - 1) * blk:(i + 1) * blk, kcols]
                vwin = vc_ref[(i - 1) * blk:(i + 1) * blk, kcols]
                bias = bias_rest
            parts, sinks = [], []
            for j in range(pairs):
                col0 = (kvh * pairs + j) * LANES
                qj = q_ref[qrows, col0:col0 + LANES]
                parts += [qj * m_lo, qj * m_hi]
                head = kvh * SW_GROUP + 2 * j
                sinks += [jnp.full((blk, 1), sink_ref[head], F32),
                          jnp.full((blk, 1), sink_ref[head + 1], F32)]
            q8 = jnp.concatenate(parts, axis=0)
            sink = jnp.concatenate(sinks, axis=0)
            s = lax.dot_general(q8, kwin, _NT, preferred_element_type=F32) + bias
            mx = jnp.maximum(jnp.max(s, axis=-1, keepdims=True), sink)
            e = jnp.exp(s - mx)
            denom = jnp.sum(e, axis=-1, keepdims=True) + jnp.exp(sink - mx)
            o8 = jnp.dot(e.astype(BF16), vwin, preferred_element_type=F32) / denom
            for j in range(pairs):
                top = o8[(2 * j) * blk:(2 * j + 1) * blk]
                bot = o8[(2 * j + 1) * blk:(2 * j + 2) * blk]
                col0 = (kvh * pairs + j) * LANES
                o_ref[qrows, col0:col0 + LANES] = jnp.where(lower, top, bot).astype(BF16)


def _swa_attn(q, kd, vd, sinks, batch, seq, tq):
    m, d = q.shape
    dk = kd.shape[1]
    nt = seq // tq
    per = tq // WINDOW

    def prev_map(b, n):
        return (jnp.maximum(b * (seq // WINDOW) + n * per - 1, 0), 0)

    cur = pl.BlockSpec((tq, dk), lambda b, n: (b * nt + n, 0))
    prev = pl.BlockSpec((WINDOW, dk), prev_map)
    return pl.pallas_call(
        functools.partial(_swa_kernel, tq=tq),
        grid=(batch, nt),
        in_specs=[
            pl.BlockSpec(memory_space=pltpu.SMEM),
            pl.BlockSpec((tq, d), lambda b, n: (b * nt + n, 0)),
            cur, prev, cur, prev,
        ],
        out_specs=pl.BlockSpec((tq, d), lambda b, n: (b * nt + n, 0)),
        out_shape=jax.ShapeDtypeStruct((m, d), BF16),
        compiler_params=_params("parallel", "parallel"),
        name="swa_attn",
    )(sinks, q, kd, kd, vd, vd)


def _layer_norm(z, g, b):
    mu = jnp.mean(z, axis=-1, keepdims=True)
    zc = z - mu
    var = jnp.mean(zc * zc, axis=-1, keepdims=True)
    return zc * lax.rsqrt(var + LN_EPS) * g + b


def _proj_ln_kernel(o_ref, w_ref, b_ref, x_ref, gate_ref, g_ref, beta_ref, out_ref, *, alpha):
    y = jnp.dot(o_ref[...], w_ref[...], preferred_element_type=F32) + b_ref[...]
    z = alpha * x_ref[...] + (1.0 + gate_ref[...]) * y
    out_ref[...] = _layer_norm(z, g_ref[...], beta_ref[...])


def _proj_ln(o, w_bf, bias, x2, mod_l, ln_g, ln_b, alpha, seq, bm):
    m, d = x2.shape
    tpb = seq // bm
    row = pl.BlockSpec((1, d), lambda i: (0, 0))
    return pl.pallas_call(
        functools.partial(_proj_ln_kernel, alpha=alpha),
        grid=(m // bm,),
        in_specs=[
            pl.BlockSpec((bm, d), lambda i: (i, 0)),
            pl.BlockSpec((d, d), lambda i: (0, 0)),
            row,
            pl.BlockSpec((bm, d), lambda i: (i, 0)),
            pl.BlockSpec((None, 1, d), lambda i: (i // tpb, 0, G_A)),
            row, row,
        ],
        out_specs=pl.BlockSpec((bm, d), lambda i: (i, 0)),
        out_shape=jax.ShapeDtypeStruct((m, d), F32),
        compiler_params=_params("parallel"),
        name="proj_ln",
    )(o, w_bf, bias.reshape(1, d), x2, mod_l, ln_g.reshape(1, d), ln_b.reshape(1, d))


def _mlp_kernel(x_ref, sc_ref, sh_ref, gate_ref, wu_ref, wd_ref, g_ref, beta_ref, out_ref, h_ref, *, alpha):
    k = pl.program_id(1)

    @pl.when(k == 0)
    def _():
        h_ref[...] = _modulate(x_ref, sc_ref, sh_ref)
        out_ref[...] = jnp.zeros(out_ref.shape, F32)

    u = jnp.maximum(jnp.dot(h_ref[...], wu_ref[...], preferred_element_type=F32), 0.0)
    out_ref[...] += jnp.dot((u * u).astype(BF16), wd_ref[...], preferred_element_type=F32)

    @pl.when(k == pl.num_programs(1) - 1)
    def _():
        z = alpha * x_ref[...] + (1.0 + gate_ref[...]) * out_ref[...]
        out_ref[...] = _layer_norm(z, g_ref[...], beta_ref[...])


def _mlp_ln(x2, mod_l, wu_bf, wd_bf, ln_g, ln_b, alpha, seq, bm, fc):
    m, d = x2.shape
    f = wu_bf.shape[1]
    tpb = seq // bm
    row = pl.BlockSpec((1, d), lambda i, k: (0, 0))

    def mod(which):
        return pl.BlockSpec((None, 1, d), lambda i, k: (i // tpb, 0, which))

    return pl.pallas_call(
        functools.partial(_mlp_kernel, alpha=alpha),
        grid=(m // bm, f // fc),
        in_specs=[
            pl.BlockSpec((bm, d), lambda i, k: (i, 0)),
            mod(SC_M), mod(SH_M), mod(G_M),
            pl.BlockSpec((d, fc), lambda i, k: (0, k)),
            pl.BlockSpec((fc, d), lambda i, k: (k, 0)),
            row, row,
        ],
        out_specs=pl.BlockSpec((bm, d), lambda i, k: (i, 0)),
        out_shape=jax.ShapeDtypeStruct((m, d), F32),
        scratch_shapes=[pltpu.VMEM((bm, d), BF16)],
        compiler_params=_params("parallel", "arbitrary"),
        name="mlp_ln",
    )(x2, mod_l, mod_l, mod_l, wu_bf, wd_bf, ln_g.reshape(1, d), ln_b.reshape(1, d))


def kernel(x, c, w_ada, b_ada, ln_mix_g, ln_mix_b, ln_mlp_g, ln_mlp_b, w_up, w_down, a_w_qkv, a_w_o, a_lambda_q1, a_lambda_k1, a_lambda_q2, a_lambda_k2, a_subln_g, b_w_qkv, b_b_qkv, b_sinks, b_w_o, b_b_o):
    batch, seq, d = x.shape
    depth = w_ada.shape[0]
    alpha = (2 * depth) ** 0.25
    m = batch * seq
    bm = min(512, seq)
    bm_mlp = min(1024, seq)
    fc = min(512, w_up.shape[-1])
    blk = min(512, seq)
    tq = min(512, seq)

    mod = _ada_mod(c, w_ada, b_ada)
    tabs_a = _rope_tables(seq, DA_HEAD_DIM, DA_HEAD_DIM ** -0.5)
    tabs_b = _rope_tables(seq, SW_HEAD_DIM, SW_HEAD_DIM ** -0.5)
    zero_bias = jnp.zeros((d,), F32)

    x2 = x.reshape(m, d)
    for i in range(depth):
        j = i // 2
        mod_l = mod[i]
        if i % 2 == 0:
            qkv = _qkv_diff(x2, mod_l, a_w_qkv[j].astype(BF16), tabs_a, seq, bm)
            lam_vecs = (a_lambda_q1[j], a_lambda_k1[j], a_lambda_q2[j], a_lambda_k2[j])
            o = _diff_attn(qkv, lam_vecs, a_subln_g[j], i, batch, seq, blk)
            x2 = _proj_ln(o, a_w_o[j].astype(BF16), zero_bias, x2, mod_l, ln_mix_g[i], ln_mix_b[i], alpha, seq, bm)
        else:
            q, kd, vd = _qkv_swa(x2, mod_l, b_w_qkv[j].astype(BF16), b_b_qkv[j], tabs_b, seq, bm)
            o = _swa_attn(q, kd, vd, b_sinks[j], batch, seq, tq)
            x2 = _proj_ln(o, b_w_o[j].astype(BF16), b_b_o[j], x2, mod_l, ln_mix_g[i], ln_mix_b[i], alpha, seq, bm)
        x2 = _mlp_ln(x2, mod_l, w_up[i].astype(BF16), w_down[i].astype(BF16), ln_mlp_g[i], ln_mlp_b[i],
                     alpha, seq, bm_mlp, fc)
    return x2.reshape(batch, seq, d)
```

```python
import functools
import math

import jax
import jax.numpy as jnp
from jax import lax
from jax.experimental import pallas as pl
from jax.experimental.pallas import tpu as pltpu

F32 = jnp.float32
BF16 = jnp.bfloat16

LANES = 128
DA_HEAD_DIM = 128
SW_HEAD_DIM = 64
SW_GROUP = 8
STRIP = 256
WINDOW = 128
N_MOD = 6
ROPE_THETA = 10000.0
LN_EPS = 1e-5
LOG2E = math.log2(math.e)
VMEM_LIMIT_BYTES = 56 * 1024 * 1024

SH_A, SC_A, G_A, SH_M, SC_M, G_M = range(N_MOD)

_NT = (((1,), (1,)), ((), ()))
_TN = (((0,), (0,)), ((), ()))


def _params(*sem):
    return pltpu.CompilerParams(dimension_semantics=sem, vmem_limit_bytes=VMEM_LIMIT_BYTES)


def _ada_kernel(c_ref, w_ref, b_ref, o_ref):
    c = c_ref[...]
    ca = (c * (1.0 / (1.0 + jnp.exp(-c)))).astype(BF16)
    y = jnp.dot(ca, w_ref[...].astype(BF16), preferred_element_type=F32)
    o_ref[...] = y + b_ref[...]


def _ada_mod(c, w_ada, b_ada):
    depth, d, n = w_ada.shape
    b = c.shape[0]
    rows = 8
    bn = min(n, 1024)
    c8 = jnp.zeros((rows, d), F32).at[:b].set(c)
    out = pl.pallas_call(
        _ada_kernel,
        grid=(depth, n // bn),
        in_specs=[
            pl.BlockSpec((rows, d), lambda l, j: (0, 0)),
            pl.BlockSpec((None, d, bn), lambda l, j: (l, 0, j)),
            pl.BlockSpec((None, 1, bn), lambda l, j: (l, 0, j)),
        ],
        out_specs=pl.BlockSpec((None, rows, bn), lambda l, j: (l, 0, j)),
        out_shape=jax.ShapeDtypeStruct((depth, rows, n), F32),
        compiler_params=_params("parallel", "parallel"),
        name="ada_mod",
    )(c8, w_ada, b_ada.reshape(depth, 1, n))
    return out[:, :b].reshape(depth, b, 1, n)


def _rope_tables(seq, dim, scale):
    inv = 1.0 / (ROPE_THETA ** (jnp.arange(0, dim, 2, dtype=F32) / dim))
    ang = jnp.arange(seq, dtype=F32)[:, None] * inv[None, :]
    cos, sin = jnp.cos(ang), jnp.sin(ang)
    reps = LANES // dim
    cos_t = jnp.tile(jnp.concatenate([cos, cos], -1), (1, reps))
    sin_t = jnp.tile(jnp.concatenate([-sin, sin], -1), (1, reps))
    return jnp.stack([cos_t * scale, sin_t * scale, cos_t, sin_t])


def _rope(x, cos, sin, half):
    if 2 * half == LANES:
        partner = pltpu.roll(x, half, axis=1)
    else:
        lane = lax.broadcasted_iota(jnp.int32, x.shape, 1)
        lower = (lane % (2 * half)) < half
        partner = jnp.where(lower, pltpu.roll(x, LANES - half, axis=1), pltpu.roll(x, half, axis=1))
    return x * cos + partner * sin


def _modulate(x_ref, sc_ref, sh_ref):
    return (x_ref[...] * (1.0 + sc_ref[...]) + sh_ref[...]).astype(BF16)


def _qkv_diff_kernel(x_ref, sc_ref, sh_ref, w_ref, tab_ref, o_ref):
    j = pl.program_id(0)
    y = jnp.dot(_modulate(x_ref, sc_ref, sh_ref), w_ref[...], preferred_element_type=F32)

    @pl.when(j == 2)
    def _():
        o_ref[...] = y.astype(BF16)

    @pl.when(j < 2)
    def _():
        cos, sin = tab_ref[0], tab_ref[1]
        for g in range(y.shape[1] // LANES):
            cols = slice(g * LANES, (g + 1) * LANES)
            o_ref[:, cols] = _rope(y[:, cols], cos, sin, DA_HEAD_DIM // 2).astype(BF16)


def _qkv_diff(x2, mod_l, w_bf, tabs, seq, bm):
    m, d = x2.shape
    tpb = seq // bm
    return pl.pallas_call(
        _qkv_diff_kernel,
        grid=(3, m // bm),
        in_specs=[
            pl.BlockSpec((bm, d), lambda j, i: (i, 0)),
            pl.BlockSpec((None, 1, d), lambda j, i: (i // tpb, 0, SC_A)),
            pl.BlockSpec((None, 1, d), lambda j, i: (i // tpb, 0, SH_A)),
            pl.BlockSpec((d, d), lambda j, i: (0, j)),
            pl.BlockSpec((2, bm, LANES), lambda j, i: (jnp.minimum(j, 1), i % tpb, 0)),
        ],
        out_specs=pl.BlockSpec((bm, d), lambda j, i: (i, j)),
        out_shape=jax.ShapeDtypeStruct((m, 3 * d), BF16),
        compiler_params=_params("parallel", "parallel"),
        name="qkv_diff",
    )(x2, mod_l, mod_l, w_bf, tabs.reshape(4, seq, LANES))


def _diff_attn_kernel(q_ref, k_ref, v_ref, lq1_ref, lk1_ref, lq2_ref, lk2_ref, g_ref, o_ref,
                      acc_ref, s_ref, p_ref, a_ref, *, bq, bk, lam_init):
    qi = pl.program_id(2)
    d = DA_HEAD_DIM
    sw = STRIP
    ns = bq // sw
    per = bq // bk
    q = q_ref[...]
    ahead = s_ref.shape[0]
    full = [(t, c) for t in range(2) for c in range(ns)]

    def key_rows(j):
        return pl.ds(pl.multiple_of(j * bk, bk), bk)

    def qk(k, t, c):
        return lax.dot_general(k[:, t * d:(t + 1) * d], q[c * sw:(c + 1) * sw, t * d:(t + 1) * d], _NT,
                               preferred_element_type=F32)

    def pv(v, t, c, p, alpha):
        cols = slice(c * sw, (c + 1) * sw)
        acc_ref[t, :, cols] = alpha * acc_ref[t, :, cols] + lax.dot_general(v, p, _TN, preferred_element_type=F32)

    def block(j, stats, chains, prev_chain, next_chains, diag):
        alpha_prev = a_ref[...]
        k = k_ref[key_rows(j), :]
        v = v_ref[key_rows(j), :]
        n = len(chains)
        s_live, res = {}, {}
        p_prev = None
        for i, (t, c) in enumerate(chains):
            nxt = i + ahead
            if nxt < n:
                s_live[nxt] = qk(k, *chains[nxt])
            elif next_chains is not None:
                s_ref[nxt - n] = qk(k_ref[key_rows(j + 1), :], *next_chains[nxt - n])
            if i == 0:
                pv(v_ref[key_rows(jnp.maximum(j - 1, 0)), :], *prev_chain, p_ref[...], alpha_prev)
            else:
                pv(v, *chains[i - 1], p_prev, alpha_prev)

            s = s_ref[i] if i < ahead else s_live.pop(i)
            if diag is not None and (c + 1) * sw <= (diag + 1) * bk:
                kpos = lax.broadcasted_iota(jnp.int32, s.shape, 0) + diag * bk
                qpos = lax.broadcasted_iota(jnp.int32, s.shape, 1) + c * sw
                s = jnp.where(kpos <= qpos, s, -jnp.inf)
            m_prev = stats[2 * t][:, c * sw:(c + 1) * sw]
            l_prev = stats[2 * t + 1][:, c * sw:(c + 1) * sw]
            m_new = jnp.maximum(m_prev, jnp.max(s, axis=0, keepdims=True))
            alpha_prev = jnp.exp2(m_prev - m_new)
            p = jnp.exp2(s - m_new)
            res[(t, c)] = (m_new, alpha_prev * l_prev + jnp.sum(p, axis=0, keepdims=True))
            p_prev = p.astype(BF16)
        if next_chains is None:
            pv(v, *chains[n - 1], p_prev, alpha_prev)
        else:
            p_ref[...] = p_prev
            a_ref[...] = alpha_prev
        out = []
        for t in range(2):
            for which in range(2):
                out.append(jnp.concatenate(
                    [res[(t, c)][which] if (t, c) in res else stats[2 * t + which][:, c * sw:(c + 1) * sw]
                     for c in range(ns)], axis=1))
        return tuple(out)

    acc_ref[...] = jnp.zeros(acc_ref.shape, F32)
    p_ref[...] = jnp.zeros(p_ref.shape, BF16)
    a_ref[...] = jnp.ones(a_ref.shape, F32)
    k0 = k_ref[key_rows(0), :]
    for i in range(ahead):
        s_ref[i] = qk(k0, *full[i])
    neg = jnp.full((1, bq), -jnp.inf, F32)
    zero = jnp.zeros((1, bq), F32)
    stats = lax.fori_loop(0, per * qi, lambda j, st: block(j, st, full, full[-1], full, None),
                          (neg, zero, neg, zero))
    prev = full[-1]
    for dg in range(per):
        chains = [(t, c) for (t, c) in full if (c + 1) * sw > dg * bk]
        nxt = None if dg == per - 1 else [(t, c) for (t, c) in full if (c + 1) * sw > (dg + 1) * bk]
        stats = block(per * qi + dg, stats, chains, prev, nxt, dg)
        prev = chains[-1]
    _, l1, _, l2 = stats

    lam = (jnp.exp(jnp.sum(lq1_ref[...] * lk1_ref[...], keepdims=True))
           - jnp.exp(jnp.sum(lq2_ref[...] * lk2_ref[...], keepdims=True)) + lam_init)
    of = acc_ref[0] / l1 - lam * (acc_ref[1] / l2)
    of = of * lax.rsqrt(jnp.mean(of * of, axis=0, keepdims=True) + LN_EPS)
    o_ref[...] = (of.T * g_ref[...] * (1.0 - lam_init)).astype(BF16)


def _diff_attn(qkv, lam_vecs, subln_g, layer_idx, batch, seq, bq, bk):
    m, d3 = qkv.shape
    d = d3 // 3
    w = 2 * DA_HEAD_DIM
    heads = d // w
    nq = seq // bq
    assert bq % bk == 0 and bk % STRIP == 0
    lam_init = 0.8 - 0.6 * math.exp(-0.3 * layer_idx)
    vec = pl.BlockSpec((1, DA_HEAD_DIM), lambda b, h, i: (0, 0))
    return pl.pallas_call(
        functools.partial(_diff_attn_kernel, bq=bq, bk=bk, lam_init=lam_init),
        grid=(batch, heads, nq),
        in_specs=[
            pl.BlockSpec((bq, w), lambda b, h, i: (b * nq + i, h)),
            pl.BlockSpec((seq, w), lambda b, h, i: (b, heads + h)),
            pl.BlockSpec((seq, w), lambda b, h, i: (b, 2 * heads + h)),
            vec, vec, vec, vec,
            pl.BlockSpec((1, w), lambda b, h, i: (0, 0)),
        ],
        out_specs=pl.BlockSpec((bq, w), lambda b, h, i: (b * nq + i, h)),
        out_shape=jax.ShapeDtypeStruct((m, d), BF16),
        scratch_shapes=[
            pltpu.VMEM((2, w, bq), F32),
            pltpu.VMEM((2, bk, STRIP), F32),
            pltpu.VMEM((bk, STRIP), BF16),
            pltpu.VMEM((1, STRIP), F32),
        ],
        compiler_params=_params("parallel", "parallel", "arbitrary"),
        name="diff_attn",
    )(qkv, qkv, qkv, *[v.reshape(1, DA_HEAD_DIM) for v in lam_vecs], subln_g.reshape(1, w))


def _qkv_swa_kernel(x_ref, sc_ref, sh_ref, w_ref, b_ref, tab_ref, q_ref, k_ref, v_ref):
    y = jnp.dot(_modulate(x_ref, sc_ref, sh_ref), w_ref[...], preferred_element_type=F32) + b_ref[...]
    half = SW_HEAD_DIM // 2
    dq = q_ref.shape[1]
    dkv = k_ref.shape[1] // 2
    cq, sq, ck, sk = tab_ref[0], tab_ref[1], tab_ref[2], tab_ref[3]
    for g in range(dq // LANES):
        cols = slice(g * LANES, (g + 1) * LANES)
        q_ref[:, cols] = _rope(y[:, cols], cq, sq, half).astype(BF16)
    lane = lax.broadcasted_iota(jnp.int32, (x_ref.shape[0], LANES), 1)
    lower = lane < SW_HEAD_DIM

    def spread(t, ref, p):
        swapped = pltpu.roll(t, SW_HEAD_DIM, axis=1)
        ref[:, (2 * p) * LANES:(2 * p + 1) * LANES] = jnp.where(lower, t, swapped).astype(BF16)
        ref[:, (2 * p + 1) * LANES:(2 * p + 2) * LANES] = jnp.where(lower, swapped, t).astype(BF16)

    for p in range(dkv // LANES):
        kc = slice(dq + p * LANES, dq + (p + 1) * LANES)
        vc = slice(dq + dkv + p * LANES, dq + dkv + (p + 1) * LANES)
        spread(_rope(y[:, kc], ck, sk, half), k_ref, p)
        spread(y[:, vc], v_ref, p)


def _qkv_swa(x2, mod_l, w_bf, bias, tabs, seq, bm):
    m, d = x2.shape
    n = w_bf.shape[1]
    dkv = (n - d) // 2
    assert dkv % LANES == 0
    tpb = seq // bm
    kv_shape = jax.ShapeDtypeStruct((m, 2 * dkv), BF16)
    return pl.pallas_call(
        _qkv_swa_kernel,
        grid=(m // bm,),
        in_specs=[
            pl.BlockSpec((bm, d), lambda i: (i, 0)),
            pl.BlockSpec((None, 1, d), lambda i: (i // tpb, 0, SC_A)),
            pl.BlockSpec((None, 1, d), lambda i: (i // tpb, 0, SH_A)),
            pl.BlockSpec((d, n), lambda i: (0, 0)),
            pl.BlockSpec((1, n), lambda i: (0, 0)),
            pl.BlockSpec((4, bm, LANES), lambda i: (0, i % tpb, 0)),
        ],
        out_specs=[
            pl.BlockSpec((bm, d), lambda i: (i, 0)),
            pl.BlockSpec((bm, 2 * dkv), lambda i: (i, 0)),
            pl.BlockSpec((bm, 2 * dkv), lambda i: (i, 0)),
        ],
        out_shape=[jax.ShapeDtypeStruct((m, d), BF16), kv_shape, kv_shape],
        compiler_params=_params("parallel"),
        name="qkv_swa",
    )(x2, mod_l, mod_l, w_bf, bias.reshape(1, n), tabs)


def _swa_kernel(sink_ref, q_ref, kc_ref, kp_ref, vc_ref, vp_ref, o_ref, *, tq):
    n = pl.program_id(1)
    blk = WINDOW
    pairs = SW_GROUP // 2
    kvh_count = kc_ref.shape[1] // LANES
    rows8 = SW_GROUP * blk

    r = lax.broadcasted_iota(jnp.int32, (rows8, 2 * blk), 0) % blk
    c = lax.broadcasted_iota(jnp.int32, (rows8, 2 * blk), 1)
    delta = c - r - 1
    neg = jnp.float32(-jnp.inf)
    bias_rest = jnp.where((delta >= 0) & (delta < blk), 0.0, neg)
    lo_bound = jnp.where(n == 0, blk, 0)
    bias_first = jnp.where(c >= lo_bound, bias_rest, neg)

    lane = lax.broadcasted_iota(jnp.int32, (1, LANES), 1)
    m_lo = jnp.where(lane < SW_HEAD_DIM, 1.0, 0.0).astype(BF16)
    m_hi = jnp.where(lane < SW_HEAD_DIM, 0.0, 1.0).astype(BF16)
    lower = lax.broadcasted_iota(jnp.int32, (blk, LANES), 1) < SW_HEAD_DIM

    for i in range(tq // blk):
        qrows = slice(i * blk, (i + 1) * blk)
        for kvh in range(kvh_count):
            kcols = slice(kvh * LANES, (kvh + 1) * LANES)
            if i == 0:
                kwin = jnp.concatenate([kp_ref[:, kcols], kc_ref[0:blk, kcols]], axis=0)
                vwin = jnp.concatenate([vp_ref[:, kcols], vc_ref[0:blk, kcols]], axis=0)
                bias = bias_first
            else:
                kwin = kc_ref[(i - 1) * blk:(i + 1) * blk, kcols]
                vwin = vc_ref[(i - 1) * blk:(i + 1) * blk, kcols]
                bias = bias_rest
            parts, sinks = [], []
            for j in range(pairs):
                col0 = (kvh * pairs + j) * LANES
                qj = q_ref[qrows, col0:col0 + LANES]
                parts += [qj * m_lo, qj * m_hi]
                head = kvh * SW_GROUP + 2 * j
                sinks += [jnp.full((blk, 1), sink_ref[head], F32),
                          jnp.full((blk, 1), sink_ref[head + 1], F32)]
            q8 = jnp.concatenate(parts, axis=0)
            sink = jnp.concatenate(sinks, axis=0)
            s = lax.dot_general(q8, kwin, _NT, preferred_element_type=F32) + bias
            mx = jnp.maximum(jnp.max(s, axis=-1, keepdims=True), sink)
            e = jnp.exp(s - mx)
            denom = jnp.sum(e, axis=-1, keepdims=True) + jnp.exp(sink - mx)
            o8 = jnp.dot(e.astype(BF16), vwin, preferred_element_type=F32) / denom
            for j in range(pairs):
                top = o8[(2 * j) * blk:(2 * j + 1) * blk]
                bot = o8[(2 * j + 1) * blk:(2 * j + 2) * blk]
                col0 = (kvh * pairs + j) * LANES
                o_ref[qrows, col0:col0 + LANES] = jnp.where(lower, top, bot).astype(BF16)


def _swa_attn(q, kd, vd, sinks, batch, seq, tq):
    m, d = q.shape
    dk = kd.shape[1]
    nt = seq // tq
    per = tq // WINDOW

    def prev_map(b, n):
        return (jnp.maximum(b * (seq // WINDOW) + n * per - 1, 0), 0)

    cur = pl.BlockSpec((tq, dk), lambda b, n: (b * nt + n, 0))
    prev = pl.BlockSpec((WINDOW, dk), prev_map)
    return pl.pallas_call(
        functools.partial(_swa_kernel, tq=tq),
        grid=(batch, nt),
        in_specs=[
            pl.BlockSpec(memory_space=pltpu.SMEM),
            pl.BlockSpec((tq, d), lambda b, n: (b * nt + n, 0)),
            cur, prev, cur, prev,
        ],
        out_specs=pl.BlockSpec((tq, d), lambda b, n: (b * nt + n, 0)),
        out_shape=jax.ShapeDtypeStruct((m, d), BF16),
        compiler_params=_params("parallel", "parallel"),
        name="swa_attn",
    )(sinks, q, kd, kd, vd, vd)


def _layer_norm(z, g, b):
    mu = jnp.mean(z, axis=-1, keepdims=True)
    zc = z - mu
    var = jnp.mean(zc * zc, axis=-1, keepdims=True)
    return zc * lax.rsqrt(var + LN_EPS) * g + b


def _proj_ln_kernel(o_ref, w_ref, b_ref, x_ref, gate_ref, g_ref, beta_ref, out_ref, *, alpha):
    y = jnp.dot(o_ref[...], w_ref[...], preferred_element_type=F32) + b_ref[...]
    z = alpha * x_ref[...] + (1.0 + gate_ref[...]) * y
    out_ref[...] = _layer_norm(z, g_ref[...], beta_ref[...])


def _proj_ln(o, w_bf, bias, x2, mod_l, ln_g, ln_b, alpha, seq, bm):
    m, d = x2.shape
    tpb = seq // bm
    row = pl.BlockSpec((1, d), lambda i: (0, 0))
    return pl.pallas_call(
        functools.partial(_proj_ln_kernel, alpha=alpha),
        grid=(m // bm,),
        in_specs=[
            pl.BlockSpec((bm, d), lambda i: (i, 0)),
            pl.BlockSpec((d, d), lambda i: (0, 0)),
            row,
            pl.BlockSpec((bm, d), lambda i: (i, 0)),
            pl.BlockSpec((None, 1, d), lambda i: (i // tpb, 0, G_A)),
            row, row,
        ],
        out_specs=pl.BlockSpec((bm, d), lambda i: (i, 0)),
        out_shape=jax.ShapeDtypeStruct((m, d), F32),
        compiler_params=_params("parallel"),
        name="proj_ln",
    )(o, w_bf, bias.reshape(1, d), x2, mod_l, ln_g.reshape(1, d), ln_b.reshape(1, d))


def _mlp_kernel(x_ref, sc_ref, sh_ref, gate_ref, wu_ref, wd_ref, g_ref, beta_ref, out_ref, h_ref, *, alpha):
    k = pl.program_id(1)

    @pl.when(k == 0)
    def _():
        h_ref[...] = _modulate(x_ref, sc_ref, sh_ref)
        out_ref[...] = jnp.zeros(out_ref.shape, F32)

    u = jnp.maximum(jnp.dot(h_ref[...], wu_ref[...], preferred_element_type=F32), 0.0)
    out_ref[...] += jnp.dot((u * u).astype(BF16), wd_ref[...], preferred_element_type=F32)

    @pl.when(k == pl.num_programs(1) - 1)
    def _():
        z = alpha * x_ref[...] + (1.0 + gate_ref[...]) * out_ref[...]
        out_ref[...] = _layer_norm(z, g_ref[...], beta_ref[...])


def _mlp_ln(x2, mod_l, wu_bf, wd_bf, ln_g, ln_b, alpha, seq, bm, fc):
    m, d = x2.shape
    f = wu_bf.shape[1]
    tpb = seq // bm
    row = pl.BlockSpec((1, d), lambda i, k: (0, 0))

    def mod(which):
        return pl.BlockSpec((None, 1, d), lambda i, k: (i // tpb, 0, which))

    return pl.pallas_call(
        functools.partial(_mlp_kernel, alpha=alpha),
        grid=(m // bm, f // fc),
        in_specs=[
            pl.BlockSpec((bm, d), lambda i, k: (i, 0)),
            mod(SC_M), mod(SH_M), mod(G_M),
            pl.BlockSpec((d, fc), lambda i, k: (0, k)),
            pl.BlockSpec((fc, d), lambda i, k: (k, 0)),
            row, row,
        ],
        out_specs=pl.BlockSpec((bm, d), lambda i, k: (i, 0)),
        out_shape=jax.ShapeDtypeStruct((m, d), F32),
        scratch_shapes=[pltpu.VMEM((bm, d), BF16)],
        compiler_params=_params("parallel", "arbitrary"),
        name="mlp_ln",
    )(x2, mod_l, mod_l, mod_l, wu_bf, wd_bf, ln_g.reshape(1, d), ln_b.reshape(1, d))


def kernel(x, c, w_ada, b_ada, ln_mix_g, ln_mix_b, ln_mlp_g, ln_mlp_b, w_up, w_down, a_w_qkv, a_w_o, a_lambda_q1, a_lambda_k1, a_lambda_q2, a_lambda_k2, a_subln_g, b_w_qkv, b_b_qkv, b_sinks, b_w_o, b_b_o):
    batch, seq, d = x.shape
    depth = w_ada.shape[0]
    alpha = (2 * depth) ** 0.25
    m = batch * seq
    bm = min(512, seq)
    bm_mlp = min(1024, seq)
    fc = min(512, w_up.shape[-1])
    bq_att = min(1024, seq)
    bk_att = min(512, seq)
    tq = min(512, seq)

    mod = _ada_mod(c, w_ada, b_ada)
    tabs_a = _rope_tables(seq, DA_HEAD_DIM, DA_HEAD_DIM ** -0.5 * LOG2E)
    tabs_b = _rope_tables(seq, SW_HEAD_DIM, SW_HEAD_DIM ** -0.5)
    zero_bias = jnp.zeros((d,), F32)

    x2 = x.reshape(m, d)
    for i in range(depth):
        j = i // 2
        mod_l = mod[i]
        if i % 2 == 0:
            qkv = _qkv_diff(x2, mod_l, a_w_qkv[j].astype(BF16), tabs_a, seq, bm)
            lam_vecs = (a_lambda_q1[j], a_lambda_k1[j], a_lambda_q2[j], a_lambda_k2[j])
            o = _diff_attn(qkv, lam_vecs, a_subln_g[j], i, batch, seq, bq_att, bk_att)
            x2 = _proj_ln(o, a_w_o[j].astype(BF16), zero_bias, x2, mod_l, ln_mix_g[i], ln_mix_b[i], alpha, seq, bm)
        else:
            q, kd, vd = _qkv_swa(x2, mod_l, b_w_qkv[j].astype(BF16), b_b_qkv[j], tabs_b, seq, bm)
            o = _swa_attn(q, kd, vd, b_sinks[j], batch, seq, tq)
            x2 = _proj_ln(o, b_w_o[j].astype(BF16), b_b_o[j], x2, mod_l, ln_mix_g[i], ln_mix_b[i], alpha, seq, bm)
        x2 = _mlp_ln(x2, mod_l, w_up[i].astype(BF16), w_down[i].astype(BF16), ln_mlp_g[i], ln_mlp_b[i],
                     alpha, seq, bm_mlp, fc)
    return x2.reshape(batch, seq, d)
```

```python
import functools
import math

import jax
import jax.numpy as jnp
from jax import lax
from jax.experimental import pallas as pl
from jax.experimental.pallas import tpu as pltpu

F32 = jnp.float32
BF16 = jnp.bfloat16

LANES = 128
DA_HEAD_DIM = 128
SW_HEAD_DIM = 64
SW_GROUP = 8
ROW_CHUNK = 256
STRIP = 256
WINDOW = 128
N_MOD = 6
ROPE_THETA = 10000.0
LN_EPS = 1e-5
LOG2E = math.log2(math.e)
VMEM_LIMIT_BYTES = 56 * 1024 * 1024

SH_A, SC_A, G_A, SH_M, SC_M, G_M = range(N_MOD)

_NT = (((1,), (1,)), ((), ()))
_TN = (((0,), (0,)), ((), ()))


def _params(*sem):
    return pltpu.CompilerParams(dimension_semantics=sem, vmem_limit_bytes=VMEM_LIMIT_BYTES)


def _ada_kernel(c_ref, w_ref, b_ref, o_ref):
    c = c_ref[...]
    ca = (c * (1.0 / (1.0 + jnp.exp(-c)))).astype(BF16)
    y = jnp.dot(ca, w_ref[...].astype(BF16), preferred_element_type=F32)
    o_ref[...] = y + b_ref[...]


def _ada_mod(c, w_ada, b_ada):
    depth, d, n = w_ada.shape
    b = c.shape[0]
    rows = 8
    bn = min(n, 1024)
    c8 = jnp.zeros((rows, d), F32).at[:b].set(c)
    out = pl.pallas_call(
        _ada_kernel,
        grid=(depth, n // bn),
        in_specs=[
            pl.BlockSpec((rows, d), lambda l, j: (0, 0)),
            pl.BlockSpec((None, d, bn), lambda l, j: (l, 0, j)),
            pl.BlockSpec((None, 1, bn), lambda l, j: (l, 0, j)),
        ],
        out_specs=pl.BlockSpec((None, rows, bn), lambda l, j: (l, 0, j)),
        out_shape=jax.ShapeDtypeStruct((depth, rows, n), F32),
        compiler_params=_params("parallel", "parallel"),
        name="ada_mod",
    )(c8, w_ada, b_ada.reshape(depth, 1, n))
    return out[:, :b].reshape(depth, b, 1, n)


def _rope_tables(seq, dim, scale):
    inv = 1.0 / (ROPE_THETA ** (jnp.arange(0, dim, 2, dtype=F32) / dim))
    ang = jnp.arange(seq, dtype=F32)[:, None] * inv[None, :]
    cos, sin = jnp.cos(ang), jnp.sin(ang)
    reps = LANES // dim
    cos_t = jnp.tile(jnp.concatenate([cos, cos], -1), (1, reps))
    sin_t = jnp.tile(jnp.concatenate([-sin, sin], -1), (1, reps))
    return jnp.stack([cos_t * scale, sin_t * scale, cos_t, sin_t])


def _rope(x, cos, sin, half):
    if 2 * half == LANES:
        partner = pltpu.roll(x, half, axis=1)
    else:
        lane = lax.broadcasted_iota(jnp.int32, x.shape, 1)
        lower = (lane % (2 * half)) < half
        partner = jnp.where(lower, pltpu.roll(x, LANES - half, axis=1), pltpu.roll(x, half, axis=1))
    return x * cos + partner * sin


def _qkv_diff_kernel(x_ref, sc_ref, sh_ref, w_ref, tab_ref, o_ref):
    is_v = pl.program_id(0) == 2
    scale = 1.0 + sc_ref[...]
    shift = sh_ref[...]
    w = w_ref[...]
    for r0 in range(0, x_ref.shape[0], ROW_CHUNK):
        rows = slice(r0, r0 + ROW_CHUNK)
        h = (x_ref[rows, :] * scale + shift).astype(BF16)
        y = jnp.dot(h, w, preferred_element_type=F32)
        cos, sin = tab_ref[0, rows, :], tab_ref[1, rows, :]
        for g in range(y.shape[1] // LANES):
            cols = slice(g * LANES, (g + 1) * LANES)
            roped = _rope(y[:, cols], cos, sin, DA_HEAD_DIM // 2)
            o_ref[rows, cols] = jnp.where(is_v, y[:, cols], roped).astype(BF16)


def _qkv_diff(x2, mod_l, w_bf, tabs, seq, bm):
    m, d = x2.shape
    tpb = seq // bm
    return pl.pallas_call(
        _qkv_diff_kernel,
        grid=(3, m // bm),
        in_specs=[
            pl.BlockSpec((bm, d), lambda j, i: (i, 0)),
            pl.BlockSpec((None, 1, d), lambda j, i: (i // tpb, 0, SC_A)),
            pl.BlockSpec((None, 1, d), lambda j, i: (i // tpb, 0, SH_A)),
            pl.BlockSpec((d, d), lambda j, i: (0, j)),
            pl.BlockSpec((2, bm, LANES), lambda j, i: (jnp.minimum(j, 1), i % tpb, 0)),
        ],
        out_specs=pl.BlockSpec((bm, d), lambda j, i: (i, j)),
        out_shape=jax.ShapeDtypeStruct((m, 3 * d), BF16),
        compiler_params=_params("parallel", "parallel"),
        name="qkv_diff",
    )(x2, mod_l, mod_l, w_bf, tabs)


def _diff_attn_kernel(q_ref, k_ref, v_ref, lq1_ref, lk1_ref, lq2_ref, lk2_ref, g_ref, o_ref,
                      acc_ref, s_ref, p_ref, a_ref, *, bq, bk, lam_init):
    qi = pl.program_id(2)
    d = DA_HEAD_DIM
    sw = STRIP
    ns = bq // sw
    per = bq // bk
    q = q_ref[...]
    ahead = s_ref.shape[0]
    full = [(t, c) for t in range(2) for c in range(ns)]

    def key_rows(j):
        return pl.ds(pl.multiple_of(j * bk, bk), bk)

    def qk(k, t, c):
        return lax.dot_general(k[:, t * d:(t + 1) * d], q[c * sw:(c + 1) * sw, t * d:(t + 1) * d], _NT,
                               preferred_element_type=F32)

    def pv(v, t, c, p, alpha):
        cols = slice(c * sw, (c + 1) * sw)
        acc_ref[t, :, cols] = alpha * acc_ref[t, :, cols] + lax.dot_general(v, p, _TN, preferred_element_type=F32)

    def block(j, stats, chains, prev_chain, next_chains, diag):
        alpha_prev = a_ref[...]
        k = k_ref[key_rows(j), :]
        v = v_ref[key_rows(j), :]
        n = len(chains)
        s_live, res = {}, {}
        p_prev = None
        for i, (t, c) in enumerate(chains):
            nxt = i + ahead
            if nxt < n:
                s_live[nxt] = qk(k, *chains[nxt])
            elif next_chains is not None:
                s_ref[nxt - n] = qk(k_ref[key_rows(j + 1), :], *next_chains[nxt - n])
            if i == 0:
                pv(v_ref[key_rows(jnp.maximum(j - 1, 0)), :], *prev_chain, p_ref[...], alpha_prev)
            else:
                pv(v, *chains[i - 1], p_prev, alpha_prev)

            s = s_ref[i] if i < ahead else s_live.pop(i)
            if diag is not None and (c + 1) * sw <= (diag + 1) * bk:
                kpos = lax.broadcasted_iota(jnp.int32, s.shape, 0) + diag * bk
                qpos = lax.broadcasted_iota(jnp.int32, s.shape, 1) + c * sw
                s = jnp.where(kpos <= qpos, s, -jnp.inf)
            m_prev = stats[2 * t][:, c * sw:(c + 1) * sw]
            l_prev = stats[2 * t + 1][:, c * sw:(c + 1) * sw]
            m_new = jnp.maximum(m_prev, jnp.max(s, axis=0, keepdims=True))
            alpha_prev = jnp.exp2(m_prev - m_new)
            p = jnp.exp2(s - m_new)
            res[(t, c)] = (m_new, alpha_prev * l_prev + jnp.sum(p, axis=0, keepdims=True))
            p_prev = p.astype(BF16)
        if next_chains is None:
            pv(v, *chains[n - 1], p_prev, alpha_prev)
        else:
            p_ref[...] = p_prev
            a_ref[...] = alpha_prev
        out = []
        for t in range(2):
            for which in range(2):
                out.append(jnp.concatenate(
                    [res[(t, c)][which] if (t, c) in res else stats[2 * t + which][:, c * sw:(c + 1) * sw]
                     for c in range(ns)], axis=1))
        return tuple(out)

    acc_ref[...] = jnp.zeros(acc_ref.shape, F32)
    p_ref[...] = jnp.zeros(p_ref.shape, BF16)
    a_ref[...] = jnp.ones(a_ref.shape, F32)
    k0 = k_ref[key_rows(0), :]
    for i in range(ahead):
        s_ref[i] = qk(k0, *full[i])
    neg = jnp.full((1, bq), -jnp.inf, F32)
    zero = jnp.zeros((1, bq), F32)
    stats = lax.fori_loop(0, per * qi, lambda j, st: block(j, st, full, full[-1], full, None),
                          (neg, zero, neg, zero))
    prev = full[-1]
    for dg in range(per):
        chains = [(t, c) for (t, c) in full if (c + 1) * sw > dg * bk]
        nxt = None if dg == per - 1 else [(t, c) for (t, c) in full if (c + 1) * sw > (dg + 1) * bk]
        stats = block(per * qi + dg, stats, chains, prev, nxt, dg)
        prev = chains[-1]
    _, l1, _, l2 = stats

    lam = (jnp.exp(jnp.sum(lq1_ref[...] * lk1_ref[...], keepdims=True))
           - jnp.exp(jnp.sum(lq2_ref[...] * lk2_ref[...], keepdims=True)) + lam_init)
    of = acc_ref[0] / l1 - lam * (acc_ref[1] / l2)
    of = of * lax.rsqrt(jnp.mean(of * of, axis=0, keepdims=True) + LN_EPS)
    o_ref[...] = (of.T * g_ref[...] * (1.0 - lam_init)).astype(BF16)


def _diff_attn(qkv, lam_vecs, subln_g, layer_idx, batch, seq, bq, bk):
    m, d3 = qkv.shape
    d = d3 // 3
    w = 2 * DA_HEAD_DIM
    heads = d // w
    nq = seq // bq
    assert bq % bk == 0 and bk % STRIP == 0
    lam_init = 0.8 - 0.6 * math.exp(-0.3 * layer_idx)
    vec = pl.BlockSpec((1, DA_HEAD_DIM), lambda b, h, i: (0, 0))
    return pl.pallas_call(
        functools.partial(_diff_attn_kernel, bq=bq, bk=bk, lam_init=lam_init),
        grid=(batch, heads, nq),
        in_specs=[
            pl.BlockSpec((bq, w), lambda b, h, i: (b * nq + i, h)),
            pl.BlockSpec((seq, w), lambda b, h, i: (b, heads + h)),
            pl.BlockSpec((seq, w), lambda b, h, i: (b, 2 * heads + h)),
            vec, vec, vec, vec,
            pl.BlockSpec((1, w), lambda b, h, i: (0, 0)),
        ],
        out_specs=pl.BlockSpec((bq, w), lambda b, h, i: (b * nq + i, h)),
        out_shape=jax.ShapeDtypeStruct((m, d), BF16),
        scratch_shapes=[
            pltpu.VMEM((2, w, bq), F32),
            pltpu.VMEM((2, bk, STRIP), F32),
            pltpu.VMEM((bk, STRIP), BF16),
            pltpu.VMEM((1, STRIP), F32),
        ],
        compiler_params=_params("parallel", "parallel", "arbitrary"),
        name="diff_attn",
    )(qkv, qkv, qkv, *[v.reshape(1, DA_HEAD_DIM) for v in lam_vecs], subln_g.reshape(1, w))


def _qkv_swa_kernel(x_ref, sc_ref, sh_ref, w_ref, b_ref, tab_ref, q_ref, k_ref, v_ref):
    half = SW_HEAD_DIM // 2
    dq = q_ref.shape[1]
    dkv = k_ref.shape[1] // 2
    scale = 1.0 + sc_ref[...]
    shift = sh_ref[...]
    w = w_ref[...]
    lower = lax.broadcasted_iota(jnp.int32, (ROW_CHUNK, LANES), 1) < SW_HEAD_DIM

    for r0 in range(0, x_ref.shape[0], ROW_CHUNK):
        rows = slice(r0, r0 + ROW_CHUNK)
        h = (x_ref[rows, :] * scale + shift).astype(BF16)
        y = jnp.dot(h, w, preferred_element_type=F32) + b_ref[...]
        cq, sq, ck, sk = (tab_ref[t, rows, :] for t in range(4))
        for g in range(dq // LANES):
            cols = slice(g * LANES, (g + 1) * LANES)
            q_ref[rows, cols] = _rope(y[:, cols], cq, sq, half).astype(BF16)

        def spread(t, ref, p):
            swapped = pltpu.roll(t, SW_HEAD_DIM, axis=1)
            ref[rows, (2 * p) * LANES:(2 * p + 1) * LANES] = jnp.where(lower, t, swapped).astype(BF16)
            ref[rows, (2 * p + 1) * LANES:(2 * p + 2) * LANES] = jnp.where(lower, swapped, t).astype(BF16)

        for p in range(dkv // LANES):
            kc = slice(dq + p * LANES, dq + (p + 1) * LANES)
            vc = slice(dq + dkv + p * LANES, dq + dkv + (p + 1) * LANES)
            spread(_rope(y[:, kc], ck, sk, half), k_ref, p)
            spread(y[:, vc], v_ref, p)


def _qkv_swa(x2, mod_l, w_bf, bias, tabs, seq, bm):
    m, d = x2.shape
    n = w_bf.shape[1]
    dkv = (n - d) // 2
    assert dkv % LANES == 0
    tpb = seq // bm
    kv_shape = jax.ShapeDtypeStruct((m, 2 * dkv), BF16)
    return pl.pallas_call(
        _qkv_swa_kernel,
        grid=(m // bm,),
        in_specs=[
            pl.BlockSpec((bm, d), lambda i: (i, 0)),
            pl.BlockSpec((None, 1, d), lambda i: (i // tpb, 0, SC_A)),
            pl.BlockSpec((None, 1, d), lambda i: (i // tpb, 0, SH_A)),
            pl.BlockSpec((d, n), lambda i: (0, 0)),
            pl.BlockSpec((1, n), lambda i: (0, 0)),
            pl.BlockSpec((4, bm, LANES), lambda i: (0, i % tpb, 0)),
        ],
        out_specs=[
            pl.BlockSpec((bm, d), lambda i: (i, 0)),
            pl.BlockSpec((bm, 2 * dkv), lambda i: (i, 0)),
            pl.BlockSpec((bm, 2 * dkv), lambda i: (i, 0)),
        ],
        out_shape=[jax.ShapeDtypeStruct((m, d), BF16), kv_shape, kv_shape],
        compiler_params=_params("parallel"),
        name="qkv_swa",
    )(x2, mod_l, mod_l, w_bf, bias.reshape(1, n), tabs)


def _swa_kernel(sink_ref, q_ref, kc_ref, kp_ref, vc_ref, vp_ref, o_ref, *, tq):
    n = pl.program_id(1)
    blk = WINDOW
    pairs = SW_GROUP // 2
    kvh_count = kc_ref.shape[1] // LANES
    cols8 = SW_GROUP * blk

    c = lax.broadcasted_iota(jnp.int32, (2 * blk, cols8), 0)
    r = lax.broadcasted_iota(jnp.int32, (2 * blk, cols8), 1) % blk
    delta = c - r - 1
    neg = jnp.float32(-jnp.inf)
    bias_rest = jnp.where((delta >= 0) & (delta < blk), 0.0, neg)
    lo_bound = jnp.where(n == 0, blk, 0)
    bias_first = jnp.where(c >= lo_bound, bias_rest, neg)

    lane = lax.broadcasted_iota(jnp.int32, (1, LANES), 1)
    m_lo = jnp.where(lane < SW_HEAD_DIM, 1.0, 0.0).astype(BF16)
    m_hi = jnp.where(lane < SW_HEAD_DIM, 0.0, 1.0).astype(BF16)
    upper = lax.broadcasted_iota(jnp.int32, (LANES, blk), 0) < SW_HEAD_DIM

    units = [(i, kvh) for i in range(tq // blk) for kvh in range(kvh_count)]

    def scores(i, kvh):
        qrows = slice(i * blk, (i + 1) * blk)
        kcols = slice(kvh * LANES, (kvh + 1) * LANES)
        if i == 0:
            kwin = jnp.concatenate([kp_ref[:, kcols], kc_ref[0:blk, kcols]], axis=0)
        else:
            kwin = kc_ref[(i - 1) * blk:(i + 1) * blk, kcols]
        parts = []
        for j in range(pairs):
            col0 = (kvh * pairs + j) * LANES
            qj = q_ref[qrows, col0:col0 + LANES]
            parts += [qj * m_lo, qj * m_hi]
        q8 = jnp.concatenate(parts, axis=0)
        return lax.dot_general(kwin, q8, _NT, preferred_element_type=F32)

    def softmax(s, i, kvh):
        sinks = [jnp.full((1, blk), sink_ref[kvh * SW_GROUP + g] * LOG2E, F32) for g in range(SW_GROUP)]
        sink = jnp.concatenate(sinks, axis=1)
        s = s + (bias_first if i == 0 else bias_rest)
        mx = jnp.maximum(jnp.max(s, axis=0, keepdims=True), sink)
        e = jnp.exp2(s - mx)
        denom = jnp.sum(e, axis=0, keepdims=True) + jnp.exp2(sink - mx)
        return e.astype(BF16), denom

    def output(e, denom, i, kvh):
        qrows = slice(i * blk, (i + 1) * blk)
        kcols = slice(kvh * LANES, (kvh + 1) * LANES)
        if i == 0:
            vwin = jnp.concatenate([vp_ref[:, kcols], vc_ref[0:blk, kcols]], axis=0)
        else:
            vwin = vc_ref[(i - 1) * blk:(i + 1) * blk, kcols]
        o = lax.dot_general(vwin, e, _TN, preferred_element_type=F32) / denom
        for j in range(pairs):
            even = o[:, (2 * j) * blk:(2 * j + 1) * blk]
            odd = o[:, (2 * j + 1) * blk:(2 * j + 2) * blk]
            col0 = (kvh * pairs + j) * LANES
            o_ref[qrows, col0:col0 + LANES] = jnp.where(upper, even, odd).T.astype(BF16)

    s_next = scores(*units[0])
    pending = None
    for u, unit in enumerate(units):
        s_cur = s_next
        if u + 1 < len(units):
            s_next = scores(*units[u + 1])
        if pending is not None:
            output(*pending)
        pending = softmax(s_cur, *unit) + unit
    output(*pending)


def _swa_attn(q, kd, vd, sinks, batch, seq, tq):
    m, d = q.shape
    dk = kd.shape[1]
    nt = seq // tq
    per = tq // WINDOW

    def prev_map(b, n):
        return (jnp.maximum(b * (seq // WINDOW) + n * per - 1, 0), 0)

    cur = pl.BlockSpec((tq, dk), lambda b, n: (b * nt + n, 0))
    prev = pl.BlockSpec((WINDOW, dk), prev_map)
    return pl.pallas_call(
        functools.partial(_swa_kernel, tq=tq),
        grid=(batch, nt),
        in_specs=[
            pl.BlockSpec(memory_space=pltpu.SMEM),
            pl.BlockSpec((tq, d), lambda b, n: (b * nt + n, 0)),
            cur, prev, cur, prev,
        ],
        out_specs=pl.BlockSpec((tq, d), lambda b, n: (b * nt + n, 0)),
        out_shape=jax.ShapeDtypeStruct((m, d), BF16),
        compiler_params=_params("parallel", "parallel"),
        name="swa_attn",
    )(sinks, q, kd, kd, vd, vd)


def _layer_norm(z, g, b):
    mu = jnp.mean(z, axis=-1, keepdims=True)
    zc = z - mu
    var = jnp.mean(zc * zc, axis=-1, keepdims=True)
    return zc * lax.rsqrt(var + LN_EPS) * g + b


def _proj_ln_kernel(o_ref, w_ref, b_ref, x_ref, gate_ref, g_ref, beta_ref, out_ref, *, alpha):
    w = w_ref[...]
    gate = 1.0 + gate_ref[...]
    for r0 in range(0, o_ref.shape[0], ROW_CHUNK):
        rows = slice(r0, r0 + ROW_CHUNK)
        y = jnp.dot(o_ref[rows, :], w, preferred_element_type=F32) + b_ref[...]
        z = alpha * x_ref[rows, :] + gate * y
        out_ref[rows, :] = _layer_norm(z, g_ref[...], beta_ref[...])


def _proj_ln(o, w_bf, bias, x2, mod_l, ln_g, ln_b, alpha, seq, bm):
    m, d = x2.shape
    tpb = seq // bm
    row = pl.BlockSpec((1, d), lambda i: (0, 0))
    return pl.pallas_call(
        functools.partial(_proj_ln_kernel, alpha=alpha),
        grid=(m // bm,),
        in_specs=[
            pl.BlockSpec((bm, d), lambda i: (i, 0)),
            pl.BlockSpec((d, d), lambda i: (0, 0), pipeline_mode=pl.Buffered(1)),
            row,
            pl.BlockSpec((bm, d), lambda i: (i, 0)),
            pl.BlockSpec((None, 1, d), lambda i: (i // tpb, 0, G_A)),
            row, row,
        ],
        out_specs=pl.BlockSpec((bm, d), lambda i: (i, 0)),
        out_shape=jax.ShapeDtypeStruct((m, d), F32),
        compiler_params=_params("parallel"),
        name="proj_ln",
    )(o, w_bf, bias.reshape(1, d), x2, mod_l, ln_g.reshape(1, d), ln_b.reshape(1, d))


def _mlp_kernel(x_ref, sc_ref, sh_ref, gate_ref, wu_ref, wd_ref, g_ref, beta_ref, out_ref, *, alpha):
    k = pl.program_id(1)
    last = pl.num_programs(1) - 1
    scale = 1.0 + sc_ref[...]
    shift = sh_ref[...]

    def hidden_chunk(rows):
        h = (x_ref[rows, :] * scale + shift).astype(BF16)
        u = jnp.maximum(jnp.dot(h, wu_ref[...], preferred_element_type=F32), 0.0)
        return jnp.dot((u * u).astype(BF16), wd_ref[...], preferred_element_type=F32)

    everything = slice(0, x_ref.shape[0])

    @pl.when(k == 0)
    def _():
        out_ref[...] = hidden_chunk(everything)

    @pl.when((k > 0) & (k < last))
    def _():
        out_ref[...] += hidden_chunk(everything)

    @pl.when(k == last)
    def _():
        gate = 1.0 + gate_ref[...]
        for r0 in range(0, x_ref.shape[0], ROW_CHUNK):
            rows = slice(r0, r0 + ROW_CHUNK)
            y = out_ref[rows, :] + hidden_chunk(rows)
            z = alpha * x_ref[rows, :] + gate * y
            out_ref[rows, :] = _layer_norm(z, g_ref[...], beta_ref[...])


def _mlp_ln(x2, mod_l, wu_bf, wd_bf, ln_g, ln_b, alpha, seq, bm, fc):
    m, d = x2.shape
    f = wu_bf.shape[1]
    tpb = seq // bm
    row = pl.BlockSpec((1, d), lambda i, k: (0, 0))

    def mod(which):
        return pl.BlockSpec((None, 1, d), lambda i, k: (i // tpb, 0, which))

    return pl.pallas_call(
        functools.partial(_mlp_kernel, alpha=alpha),
        grid=(m // bm, f // fc),
        in_specs=[
            pl.BlockSpec((bm, d), lambda i, k: (i, 0)),
            mod(SC_M), mod(SH_M), mod(G_M),
            pl.BlockSpec((d, fc), lambda i, k: (0, k)),
            pl.BlockSpec((fc, d), lambda i, k: (k, 0)),
            row, row,
        ],
        out_specs=pl.BlockSpec((bm, d), lambda i, k: (i, 0)),
        out_shape=jax.ShapeDtypeStruct((m, d), F32),
        compiler_params=_params("parallel", "arbitrary"),
        name="mlp_ln",
    )(x2, mod_l, mod_l, mod_l, wu_bf, wd_bf, ln_g.reshape(1, d), ln_b.reshape(1, d))


def kernel(x, c, w_ada, b_ada, ln_mix_g, ln_mix_b, ln_mlp_g, ln_mlp_b, w_up, w_down, a_w_qkv, a_w_o, a_lambda_q1, a_lambda_k1, a_lambda_q2, a_lambda_k2, a_subln_g, b_w_qkv, b_b_qkv, b_sinks, b_w_o, b_b_o):
    batch, seq, d = x.shape
    depth = w_ada.shape[0]
    alpha = (2 * depth) ** 0.25
    m = batch * seq
    bm = min(512, seq)
    bm_qkv = min(1024, seq)
    bm_mlp = min(1024, seq)
    fc = min(1024, w_up.shape[-1])
    bq_att = min(1024, seq)
    bk_att = min(512, seq)
    tq = min(512, seq)

    mod = _ada_mod(c, w_ada, b_ada)
    tabs_a = _rope_tables(seq, DA_HEAD_DIM, DA_HEAD_DIM ** -0.5 * LOG2E)
    tabs_b = _rope_tables(seq, SW_HEAD_DIM, SW_HEAD_DIM ** -0.5 * LOG2E)
    zero_bias = jnp.zeros((d,), F32)

    x2 = x.reshape(m, d)
    for i in range(depth):
        j = i // 2
        mod_l = mod[i]
        if i % 2 == 0:
            qkv = _qkv_diff(x2, mod_l, a_w_qkv[j].astype(BF16), tabs_a, seq, bm_qkv)
            lam_vecs = (a_lambda_q1[j], a_lambda_k1[j], a_lambda_q2[j], a_lambda_k2[j])
            o = _diff_attn(qkv, lam_vecs, a_subln_g[j], i, batch, seq, bq_att, bk_att)
            x2 = _proj_ln(o, a_w_o[j].astype(BF16), zero_bias, x2, mod_l, ln_mix_g[i], ln_mix_b[i], alpha, seq, bm)
        else:
            q, kd, vd = _qkv_swa(x2, mod_l, b_w_qkv[j].astype(BF16), b_b_qkv[j], tabs_b, seq, bm_qkv)
            o = _swa_attn(q, kd, vd, b_sinks[j], batch, seq, tq)
            x2 = _proj_ln(o, b_w_o[j].astype(BF16), b_b_o[j], x2, mod_l, ln_mix_g[i], ln_mix_b[i], alpha, seq, bm)
        x2 = _mlp_ln(x2, mod_l, w_up[i].astype(BF16), w_down[i].astype(BF16), ln_mlp_g[i], ln_mlp_b[i],
                     alpha, seq, bm_mlp, fc)
    return x2.reshape(batch, seq, d)
```

```python
import functools
import math

import jax
import jax.numpy as jnp
from jax import lax
from jax.experimental import pallas as pl
from jax.experimental.pallas import tpu as pltpu

F32 = jnp.float32
BF16 = jnp.bfloat16

LANES = 128
BF16_ROWS = 16
DA_HEAD_DIM = 128
SW_HEAD_DIM = 64
SW_GROUP = 8
ROW_CHUNK = 256
STRIP = 256
WINDOW = 128
N_MOD = 6
ROPE_THETA = 10000.0
LN_EPS = 1e-5
LOG2E = math.log2(math.e)
VMEM_LIMIT_BYTES = 56 * 1024 * 1024

SH_A, SC_A, G_A, SH_M, SC_M, G_M = range(N_MOD)

_NT = (((1,), (1,)), ((), ()))
_TN = (((0,), (0,)), ((), ()))


def _params(*sem):
    return pltpu.CompilerParams(dimension_semantics=sem, vmem_limit_bytes=VMEM_LIMIT_BYTES)


def _ada_kernel(c_ref, w_ref, b_ref, o_ref):
    c = c_ref[...]
    ca = (c * (1.0 / (1.0 + jnp.exp(-c)))).astype(BF16)
    y = jnp.dot(ca, w_ref[...].astype(BF16), preferred_element_type=F32)
    o_ref[...] = y + b_ref[...]


def _ada_mod(c, w_ada, b_ada):
    depth, d, n = w_ada.shape
    b = c.shape[0]
    rows = 8
    bn = min(n, 1024)
    c8 = jnp.zeros((rows, d), F32).at[:b].set(c)
    out = pl.pallas_call(
        _ada_kernel,
        grid=(depth, n // bn),
        in_specs=[
            pl.BlockSpec((rows, d), lambda l, j: (0, 0)),
            pl.BlockSpec((None, d, bn), lambda l, j: (l, 0, j)),
            pl.BlockSpec((None, 1, bn), lambda l, j: (l, 0, j)),
        ],
        out_specs=pl.BlockSpec((None, rows, bn), lambda l, j: (l, 0, j)),
        out_shape=jax.ShapeDtypeStruct((depth, rows, n), F32),
        compiler_params=_params("parallel", "parallel"),
        name="ada_mod",
    )(c8, w_ada, b_ada.reshape(depth, 1, n))
    return out[:, :b].reshape(depth, b, 1, n)


def _cast_plan(weights, steps, step_of):
    arrays, in_specs, out_specs, out_shapes = [], [], [], []
    for w in weights:
        width = next(c for c in (2048, 1024, 512, 256, LANES)
                     if w.size % (c * steps) == 0 and (w.size // (c * steps)) % BF16_ROWS == 0)
        flat = w.reshape(-1, width)
        rows = flat.shape[0] // steps
        spec = pl.BlockSpec((rows, width), lambda *g: (step_of(*g), 0))
        arrays.append(flat)
        in_specs.append(spec)
        out_specs.append(spec)
        out_shapes.append(jax.ShapeDtypeStruct(flat.shape, BF16))
    return arrays, in_specs, out_specs, out_shapes


def _cast_slices(src_refs, dst_refs):
    for s, d in zip(src_refs, dst_refs):
        d[...] = s[...].astype(BF16)


def _rope_tables(seq, dim, scale):
    inv = 1.0 / (ROPE_THETA ** (jnp.arange(0, dim, 2, dtype=F32) / dim))
    ang = jnp.arange(seq, dtype=F32)[:, None] * inv[None, :]
    cos, sin = jnp.cos(ang), jnp.sin(ang)
    reps = LANES // dim
    cos_t = jnp.tile(jnp.concatenate([cos, cos], -1), (1, reps))
    sin_t = jnp.tile(jnp.concatenate([-sin, sin], -1), (1, reps))
    return jnp.stack([cos_t * scale, sin_t * scale, cos_t, sin_t])


def _rope(x, cos, sin, half):
    if 2 * half == LANES:
        partner = pltpu.roll(x, half, axis=1)
    else:
        lane = lax.broadcasted_iota(jnp.int32, x.shape, 1)
        lower = (lane % (2 * half)) < half
        partner = jnp.where(lower, pltpu.roll(x, LANES - half, axis=1), pltpu.roll(x, half, axis=1))
    return x * cos + partner * sin


def _qkv_diff_kernel(x_ref, sc_ref, sh_ref, w_ref, tab_ref, o_ref):
    is_v = pl.program_id(0) == 2
    scale = 1.0 + sc_ref[...]
    shift = sh_ref[...]
    w = w_ref[...]
    for r0 in range(0, x_ref.shape[0], ROW_CHUNK):
        rows = slice(r0, r0 + ROW_CHUNK)
        h = (x_ref[rows, :] * scale + shift).astype(BF16)
        y = jnp.dot(h, w, preferred_element_type=F32)
        cos, sin = tab_ref[0, rows, :], tab_ref[1, rows, :]
        for g in range(y.shape[1] // LANES):
            cols = slice(g * LANES, (g + 1) * LANES)
            roped = _rope(y[:, cols], cos, sin, DA_HEAD_DIM // 2)
            o_ref[rows, cols] = jnp.where(is_v, y[:, cols], roped).astype(BF16)


def _qkv_diff(x2, mod_l, w_bf, tabs, seq, bm):
    m, d = x2.shape
    tpb = seq // bm
    return pl.pallas_call(
        _qkv_diff_kernel,
        grid=(3, m // bm),
        in_specs=[
            pl.BlockSpec((bm, d), lambda j, i: (i, 0)),
            pl.BlockSpec((None, 1, d), lambda j, i: (i // tpb, 0, SC_A)),
            pl.BlockSpec((None, 1, d), lambda j, i: (i // tpb, 0, SH_A)),
            pl.BlockSpec((d, d), lambda j, i: (0, j)),
            pl.BlockSpec((2, bm, LANES), lambda j, i: (jnp.minimum(j, 1), i % tpb, 0)),
        ],
        out_specs=pl.BlockSpec((bm, d), lambda j, i: (i, j)),
        out_shape=jax.ShapeDtypeStruct((m, 3 * d), BF16),
        compiler_params=_params("parallel", "parallel"),
        name="qkv_diff",
    )(x2, mod_l, mod_l, w_bf, tabs)


def _diff_attn_kernel(*refs, bq, bk, lam_init, n_cast):
    q_ref, k_ref, v_ref, lq1_ref, lk1_ref, lq2_ref, lk2_ref, g_ref = refs[:8]
    o_ref = refs[8 + n_cast]
    acc_ref, s_ref, p_ref, a_ref = refs[9 + 2 * n_cast:]
    _cast_slices(refs[8:8 + n_cast], refs[9 + n_cast:9 + 2 * n_cast])
    _diff_attn_body(q_ref, k_ref, v_ref, lq1_ref, lk1_ref, lq2_ref, lk2_ref, g_ref, o_ref,
                    acc_ref, s_ref, p_ref, a_ref, bq=bq, bk=bk, lam_init=lam_init)


def _diff_attn_body(q_ref, k_ref, v_ref, lq1_ref, lk1_ref, lq2_ref, lk2_ref, g_ref, o_ref,
                    acc_ref, s_ref, p_ref, a_ref, *, bq, bk, lam_init):
    qi = pl.program_id(2)
    d = DA_HEAD_DIM
    sw = STRIP
    ns = bq // sw
    per = bq // bk
    q = q_ref[...]
    ahead = s_ref.shape[0]
    full = [(t, c) for t in range(2) for c in range(ns)]

    def key_rows(j):
        return pl.ds(pl.multiple_of(j * bk, bk), bk)

    def qk(k, t, c):
        return lax.dot_general(k[:, t * d:(t + 1) * d], q[c * sw:(c + 1) * sw, t * d:(t + 1) * d], _NT,
                               preferred_element_type=F32)

    def pv(v, t, c, p, alpha):
        cols = slice(c * sw, (c + 1) * sw)
        acc_ref[t, :, cols] = alpha * acc_ref[t, :, cols] + lax.dot_general(v, p, _TN, preferred_element_type=F32)

    def block(j, stats, chains, prev_chain, next_chains, diag):
        alpha_prev = a_ref[...]
        k = k_ref[key_rows(j), :]
        v = v_ref[key_rows(j), :]
        n = len(chains)
        s_live, res = {}, {}
        p_prev = None
        for i, (t, c) in enumerate(chains):
            nxt = i + ahead
            if nxt < n:
                s_live[nxt] = qk(k, *chains[nxt])
            elif next_chains is not None:
                s_ref[nxt - n] = qk(k_ref[key_rows(j + 1), :], *next_chains[nxt - n])
            if i == 0:
                pv(v_ref[key_rows(jnp.maximum(j - 1, 0)), :], *prev_chain, p_ref[...], alpha_prev)
            else:
                pv(v, *chains[i - 1], p_prev, alpha_prev)

            s = s_ref[i] if i < ahead else s_live.pop(i)
            if diag is not None and (c + 1) * sw <= (diag + 1) * bk:
                kpos = lax.broadcasted_iota(jnp.int32, s.shape, 0) + diag * bk
                qpos = lax.broadcasted_iota(jnp.int32, s.shape, 1) + c * sw
                s = jnp.where(kpos <= qpos, s, -jnp.inf)
            m_prev = stats[2 * t][:, c * sw:(c + 1) * sw]
            l_prev = stats[2 * t + 1][:, c * sw:(c + 1) * sw]
            m_new = jnp.maximum(m_prev, jnp.max(s, axis=0, keepdims=True))
            alpha_prev = jnp.exp2(m_prev - m_new)
            p = jnp.exp2(s - m_new)
            res[(t, c)] = (m_new, alpha_prev * l_prev + jnp.sum(p, axis=0, keepdims=True))
            p_prev = p.astype(BF16)
        if next_chains is None:
            pv(v, *chains[n - 1], p_prev, alpha_prev)
        else:
            p_ref[...] = p_prev
            a_ref[...] = alpha_prev
        out = []
        for t in range(2):
            for which in range(2):
                out.append(jnp.concatenate(
                    [res[(t, c)][which] if (t, c) in res else stats[2 * t + which][:, c * sw:(c + 1) * sw]
                     for c in range(ns)], axis=1))
        return tuple(out)

    acc_ref[...] = jnp.zeros(acc_ref.shape, F32)
    p_ref[...] = jnp.zeros(p_ref.shape, BF16)
    a_ref[...] = jnp.ones(a_ref.shape, F32)
    k0 = k_ref[key_rows(0), :]
    for i in range(ahead):
        s_ref[i] = qk(k0, *full[i])
    neg = jnp.full((1, bq), -jnp.inf, F32)
    zero = jnp.zeros((1, bq), F32)
    stats = lax.fori_loop(0, per * qi, lambda j, st: block(j, st, full, full[-1], full, None),
                          (neg, zero, neg, zero))
    prev = full[-1]
    for dg in range(per):
        chains = [(t, c) for (t, c) in full if (c + 1) * sw > dg * bk]
        nxt = None if dg == per - 1 else [(t, c) for (t, c) in full if (c + 1) * sw > (dg + 1) * bk]
        stats = block(per * qi + dg, stats, chains, prev, nxt, dg)
        prev = chains[-1]
    _, l1, _, l2 = stats

    lam = (jnp.exp(jnp.sum(lq1_ref[...] * lk1_ref[...], keepdims=True))
           - jnp.exp(jnp.sum(lq2_ref[...] * lk2_ref[...], keepdims=True)) + lam_init)
    of = acc_ref[0] / l1 - lam * (acc_ref[1] / l2)
    of = of * lax.rsqrt(jnp.mean(of * of, axis=0, keepdims=True) + LN_EPS)
    o_ref[...] = (of.T * g_ref[...] * (1.0 - lam_init)).astype(BF16)


def _diff_attn(qkv, lam_vecs, subln_g, layer_idx, batch, seq, bq, bk, cast_weights):
    m, d3 = qkv.shape
    d = d3 // 3
    w = 2 * DA_HEAD_DIM
    heads = d // w
    nq = seq // bq
    assert bq % bk == 0 and bk % STRIP == 0
    lam_init = 0.8 - 0.6 * math.exp(-0.3 * layer_idx)
    vec = pl.BlockSpec((1, DA_HEAD_DIM), lambda b, h, i: (0, 0))
    c_arrays, c_in, c_out, c_shapes = _cast_plan(cast_weights, batch * heads * nq,
                                                 lambda b, h, i: (b * heads + h) * nq + i)
    outs = pl.pallas_call(
        functools.partial(_diff_attn_kernel, bq=bq, bk=bk, lam_init=lam_init, n_cast=len(c_arrays)),
        grid=(batch, heads, nq),
        in_specs=[
            pl.BlockSpec((bq, w), lambda b, h, i: (b * nq + i, h)),
            pl.BlockSpec((seq, w), lambda b, h, i: (b, heads + h)),
            pl.BlockSpec((seq, w), lambda b, h, i: (b, 2 * heads + h)),
            vec, vec, vec, vec,
            pl.BlockSpec((1, w), lambda b, h, i: (0, 0)),
            *c_in,
        ],
        out_specs=[pl.BlockSpec((bq, w), lambda b, h, i: (b * nq + i, h)), *c_out],
        out_shape=[jax.ShapeDtypeStruct((m, d), BF16), *c_shapes],
        scratch_shapes=[
            pltpu.VMEM((2, w, bq), F32),
            pltpu.VMEM((2, bk, STRIP), F32),
            pltpu.VMEM((bk, STRIP), BF16),
            pltpu.VMEM((1, STRIP), F32),
        ],
        compiler_params=_params("parallel", "parallel", "arbitrary"),
        name="diff_attn",
    )(qkv, qkv, qkv, *[v.reshape(1, DA_HEAD_DIM) for v in lam_vecs], subln_g.reshape(1, w), *c_arrays)
    return outs[0], [c.reshape(wt.shape) for c, wt in zip(outs[1:], cast_weights)]


def _qkv_swa_kernel(x_ref, sc_ref, sh_ref, w_ref, b_ref, tab_ref, q_ref, k_ref, v_ref):
    half = SW_HEAD_DIM // 2
    dq = q_ref.shape[1]
    dkv = k_ref.shape[1] // 2
    scale = 1.0 + sc_ref[...]
    shift = sh_ref[...]
    w = w_ref[...]
    lower = lax.broadcasted_iota(jnp.int32, (ROW_CHUNK, LANES), 1) < SW_HEAD_DIM

    for r0 in range(0, x_ref.shape[0], ROW_CHUNK):
        rows = slice(r0, r0 + ROW_CHUNK)
        h = (x_ref[rows, :] * scale + shift).astype(BF16)
        y = jnp.dot(h, w, preferred_element_type=F32) + b_ref[...]
        cq, sq, ck, sk = (tab_ref[t, rows, :] for t in range(4))
        for g in range(dq // LANES):
            cols = slice(g * LANES, (g + 1) * LANES)
            q_ref[rows, cols] = _rope(y[:, cols], cq, sq, half).astype(BF16)

        def spread(t, ref, p):
            swapped = pltpu.roll(t, SW_HEAD_DIM, axis=1)
            ref[rows, (2 * p) * LANES:(2 * p + 1) * LANES] = jnp.where(lower, t, swapped).astype(BF16)
            ref[rows, (2 * p + 1) * LANES:(2 * p + 2) * LANES] = jnp.where(lower, swapped, t).astype(BF16)

        for p in range(dkv // LANES):
            kc = slice(dq + p * LANES, dq + (p + 1) * LANES)
            vc = slice(dq + dkv + p * LANES, dq + dkv + (p + 1) * LANES)
            spread(_rope(y[:, kc], ck, sk, half), k_ref, p)
            spread(y[:, vc], v_ref, p)


def _qkv_swa(x2, mod_l, w_bf, bias, tabs, seq, bm):
    m, d = x2.shape
    n = w_bf.shape[1]
    dkv = (n - d) // 2
    assert dkv % LANES == 0
    tpb = seq // bm
    kv_shape = jax.ShapeDtypeStruct((m, 2 * dkv), BF16)
    return pl.pallas_call(
        _qkv_swa_kernel,
        grid=(m // bm,),
        in_specs=[
            pl.BlockSpec((bm, d), lambda i: (i, 0)),
            pl.BlockSpec((None, 1, d), lambda i: (i // tpb, 0, SC_A)),
            pl.BlockSpec((None, 1, d), lambda i: (i // tpb, 0, SH_A)),
            pl.BlockSpec((d, n), lambda i: (0, 0)),
            pl.BlockSpec((1, n), lambda i: (0, 0)),
            pl.BlockSpec((4, bm, LANES), lambda i: (0, i % tpb, 0)),
        ],
        out_specs=[
            pl.BlockSpec((bm, d), lambda i: (i, 0)),
            pl.BlockSpec((bm, 2 * dkv), lambda i: (i, 0)),
            pl.BlockSpec((bm, 2 * dkv), lambda i: (i, 0)),
        ],
        out_shape=[jax.ShapeDtypeStruct((m, d), BF16), kv_shape, kv_shape],
        compiler_params=_params("parallel"),
        name="qkv_swa",
    )(x2, mod_l, mod_l, w_bf, bias.reshape(1, n), tabs)


def _swa_kernel(*refs, tq, n_cast):
    sink_ref, q_ref, kc_ref, kp_ref, vc_ref, vp_ref = refs[:6]
    o_ref = refs[6 + n_cast]
    _cast_slices(refs[6:6 + n_cast], refs[7 + n_cast:])
    n = pl.program_id(1)
    blk = WINDOW
    pairs = SW_GROUP // 2
    kvh_count = kc_ref.shape[1] // LANES
    cols8 = SW_GROUP * blk

    c = lax.broadcasted_iota(jnp.int32, (2 * blk, cols8), 0)
    r = lax.broadcasted_iota(jnp.int32, (2 * blk, cols8), 1) % blk
    delta = c - r - 1
    neg = jnp.float32(-jnp.inf)
    bias_rest = jnp.where((delta >= 0) & (delta < blk), 0.0, neg)
    lo_bound = jnp.where(n == 0, blk, 0)
    bias_first = jnp.where(c >= lo_bound, bias_rest, neg)

    lane = lax.broadcasted_iota(jnp.int32, (1, LANES), 1)
    m_lo = jnp.where(lane < SW_HEAD_DIM, 1.0, 0.0).astype(BF16)
    m_hi = jnp.where(lane < SW_HEAD_DIM, 0.0, 1.0).astype(BF16)
    upper = lax.broadcasted_iota(jnp.int32, (LANES, blk), 0) < SW_HEAD_DIM

    units = [(i, kvh) for i in range(tq // blk) for kvh in range(kvh_count)]

    def scores(i, kvh):
        qrows = slice(i * blk, (i + 1) * blk)
        kcols = slice(kvh * LANES, (kvh + 1) * LANES)
        if i == 0:
            kwin = jnp.concatenate([kp_ref[:, kcols], kc_ref[0:blk, kcols]], axis=0)
        else:
            kwin = kc_ref[(i - 1) * blk:(i + 1) * blk, kcols]
        parts = []
        for j in range(pairs):
            col0 = (kvh * pairs + j) * LANES
            qj = q_ref[qrows, col0:col0 + LANES]
            parts += [qj * m_lo, qj * m_hi]
        q8 = jnp.concatenate(parts, axis=0)
        return lax.dot_general(kwin, q8, _NT, preferred_element_type=F32)

    def softmax(s, i, kvh):
        sinks = [jnp.full((1, blk), sink_ref[kvh * SW_GROUP + g] * LOG2E, F32) for g in range(SW_GROUP)]
        sink = jnp.concatenate(sinks, axis=1)
        s = s + (bias_first if i == 0 else bias_rest)
        mx = jnp.maximum(jnp.max(s, axis=0, keepdims=True), sink)
        e = jnp.exp2(s - mx)
        denom = jnp.sum(e, axis=0, keepdims=True) + jnp.exp2(sink - mx)
        return e.astype(BF16), denom

    def output(e, denom, i, kvh):
        qrows = slice(i * blk, (i + 1) * blk)
        kcols = slice(kvh * LANES, (kvh + 1) * LANES)
        if i == 0:
            vwin = jnp.concatenate([vp_ref[:, kcols], vc_ref[0:blk, kcols]], axis=0)
        else:
            vwin = vc_ref[(i - 1) * blk:(i + 1) * blk, kcols]
        o = lax.dot_general(vwin, e, _TN, preferred_element_type=F32) / denom
        for j in range(pairs):
            even = o[:, (2 * j) * blk:(2 * j + 1) * blk]
            odd = o[:, (2 * j + 1) * blk:(2 * j + 2) * blk]
            col0 = (kvh * pairs + j) * LANES
            o_ref[qrows, col0:col0 + LANES] = jnp.where(upper, even, odd).T.astype(BF16)

    s_next = scores(*units[0])
    pending = None
    for u, unit in enumerate(units):
        s_cur = s_next
        if u + 1 < len(units):
            s_next = scores(*units[u + 1])
        if pending is not None:
            output(*pending)
        pending = softmax(s_cur, *unit) + unit
    output(*pending)


def _swa_attn(q, kd, vd, sinks, batch, seq, tq, cast_weights):
    m, d = q.shape
    dk = kd.shape[1]
    nt = seq // tq
    per = tq // WINDOW

    def prev_map(b, n):
        return (jnp.maximum(b * (seq // WINDOW) + n * per - 1, 0), 0)

    cur = pl.BlockSpec((tq, dk), lambda b, n: (b * nt + n, 0))
    prev = pl.BlockSpec((WINDOW, dk), prev_map)
    c_arrays, c_in, c_out, c_shapes = _cast_plan(cast_weights, batch * nt, lambda b, n: b * nt + n)
    outs = pl.pallas_call(
        functools.partial(_swa_kernel, tq=tq, n_cast=len(c_arrays)),
        grid=(batch, nt),
        in_specs=[
            pl.BlockSpec(memory_space=pltpu.SMEM),
            pl.BlockSpec((tq, d), lambda b, n: (b * nt + n, 0)),
            cur, prev, cur, prev,
            *c_in,
        ],
        out_specs=[pl.BlockSpec((tq, d), lambda b, n: (b * nt + n, 0)), *c_out],
        out_shape=[jax.ShapeDtypeStruct((m, d), BF16), *c_shapes],
        compiler_params=_params("parallel", "parallel"),
        name="swa_attn",
    )(sinks, q, kd, kd, vd, vd, *c_arrays)
    return outs[0], [c.reshape(wt.shape) for c, wt in zip(outs[1:], cast_weights)]


def _layer_norm(z, g, b):
    mu = jnp.mean(z, axis=-1, keepdims=True)
    zc = z - mu
    var = jnp.mean(zc * zc, axis=-1, keepdims=True)
    return zc * lax.rsqrt(var + LN_EPS) * g + b


def _proj_ln_kernel(o_ref, w_ref, b_ref, x_ref, gate_ref, g_ref, beta_ref, out_ref, *, alpha):
    w = w_ref[...]
    gate = 1.0 + gate_ref[...]
    for r0 in range(0, o_ref.shape[0], ROW_CHUNK):
        rows = slice(r0, r0 + ROW_CHUNK)
        y = jnp.dot(o_ref[rows, :], w, preferred_element_type=F32) + b_ref[...]
        z = alpha * x_ref[rows, :] + gate * y
        out_ref[rows, :] = _layer_norm(z, g_ref[...], beta_ref[...])


def _proj_ln(o, w_bf, bias, x2, mod_l, ln_g, ln_b, alpha, seq, bm):
    m, d = x2.shape
    tpb = seq // bm
    row = pl.BlockSpec((1, d), lambda i: (0, 0))
    return pl.pallas_call(
        functools.partial(_proj_ln_kernel, alpha=alpha),
        grid=(m // bm,),
        in_specs=[
            pl.BlockSpec((bm, d), lambda i: (i, 0)),
            pl.BlockSpec((d, d), lambda i: (0, 0), pipeline_mode=pl.Buffered(1)),
            row,
            pl.BlockSpec((bm, d), lambda i: (i, 0)),
            pl.BlockSpec((None, 1, d), lambda i: (i // tpb, 0, G_A)),
            row, row,
        ],
        out_specs=pl.BlockSpec((bm, d), lambda i: (i, 0)),
        out_shape=jax.ShapeDtypeStruct((m, d), F32),
        compiler_params=_params("parallel"),
        name="proj_ln",
    )(o, w_bf, bias.reshape(1, d), x2, mod_l, ln_g.reshape(1, d), ln_b.reshape(1, d))


def _mlp_kernel(x_ref, sc_ref, sh_ref, gate_ref, wu_ref, wd_ref, g_ref, beta_ref, out_ref, *, alpha):
    k = pl.program_id(1)
    last = pl.num_programs(1) - 1
    scale = 1.0 + sc_ref[...]
    shift = sh_ref[...]

    def hidden_chunk(rows):
        h = (x_ref[rows, :] * scale + shift).astype(BF16)
        u = jnp.maximum(jnp.dot(h, wu_ref[...], preferred_element_type=F32), 0.0)
        return jnp.dot((u * u).astype(BF16), wd_ref[...], preferred_element_type=F32)

    everything = slice(0, x_ref.shape[0])

    @pl.when(k == 0)
    def _():
        out_ref[...] = hidden_chunk(everything)

    @pl.when((k > 0) & (k < last))
    def _():
        out_ref[...] += hidden_chunk(everything)

    @pl.when(k == last)
    def _():
        gate = 1.0 + gate_ref[...]
        for r0 in range(0, x_ref.shape[0], ROW_CHUNK):
            rows = slice(r0, r0 + ROW_CHUNK)
            y = out_ref[rows, :] + hidden_chunk(rows)
            z = alpha * x_ref[rows, :] + gate * y
            out_ref[rows, :] = _layer_norm(z, g_ref[...], beta_ref[...])


def _mlp_ln(x2, mod_l, wu_bf, wd_bf, ln_g, ln_b, alpha, seq, bm, fc):
    m, d = x2.shape
    f = wu_bf.shape[1]
    tpb = seq // bm
    row = pl.BlockSpec((1, d), lambda i, k: (0, 0))

    def mod(which):
        return pl.BlockSpec((None, 1, d), lambda i, k: (i // tpb, 0, which))

    return pl.pallas_call(
        functools.partial(_mlp_kernel, alpha=alpha),
        grid=(m // bm, f // fc),
        in_specs=[
            pl.BlockSpec((bm, d), lambda i, k: (i, 0)),
            mod(SC_M), mod(SH_M), mod(G_M),
            pl.BlockSpec((d, fc), lambda i, k: (0, k)),
            pl.BlockSpec((fc, d), lambda i, k: (k, 0)),
            row, row,
        ],
        out_specs=pl.BlockSpec((bm, d), lambda i, k: (i, 0)),
        out_shape=jax.ShapeDtypeStruct((m, d), F32),
        compiler_params=_params("parallel", "arbitrary"),
        name="mlp_ln",
    )(x2, mod_l, mod_l, mod_l, wu_bf, wd_bf, ln_g.reshape(1, d), ln_b.reshape(1, d))


def kernel(x, c, w_ada, b_ada, ln_mix_g, ln_mix_b, ln_mlp_g, ln_mlp_b, w_up, w_down, a_w_qkv, a_w_o, a_lambda_q1, a_lambda_k1, a_lambda_q2, a_lambda_k2, a_subln_g, b_w_qkv, b_b_qkv, b_sinks, b_w_o, b_b_o):
    batch, seq, d = x.shape
    depth = w_ada.shape[0]
    alpha = (2 * depth) ** 0.25
    m = batch * seq
    bm = min(512, seq)
    bm_qkv = min(1024, seq)
    bm_mlp = min(1024, seq)
    fc = min(1024, w_up.shape[-1])
    bq_att = min(2048, seq)
    bk_att = min(512, seq)
    tq = min(512, seq)

    mod = _ada_mod(c, w_ada, b_ada)
    tabs_a = _rope_tables(seq, DA_HEAD_DIM, DA_HEAD_DIM ** -0.5 * LOG2E)
    tabs_b = _rope_tables(seq, SW_HEAD_DIM, SW_HEAD_DIM ** -0.5 * LOG2E)
    zero_bias = jnp.zeros((d,), F32)

    x2 = x.reshape(m, d)
    w_qkv_bf = a_w_qkv[0].astype(BF16)
    for i in range(depth):
        j = i // 2
        mod_l = mod[i]
        is_a = i % 2 == 0
        w_o = a_w_o[j] if is_a else b_w_o[j]
        pending = [w_o, w_up[i], w_down[i]]
        if i + 1 < depth:
            pending.append(b_w_qkv[(i + 1) // 2] if is_a else a_w_qkv[(i + 1) // 2])
        if is_a:
            qkv = _qkv_diff(x2, mod_l, w_qkv_bf, tabs_a, seq, bm_qkv)
            lam_vecs = (a_lambda_q1[j], a_lambda_k1[j], a_lambda_q2[j], a_lambda_k2[j])
            o, cast = _diff_attn(qkv, lam_vecs, a_subln_g[j], i, batch, seq, bq_att, bk_att, pending)
            bias_o = zero_bias
        else:
            q, kd, vd = _qkv_swa(x2, mod_l, w_qkv_bf, b_b_qkv[j], tabs_b, seq, bm_qkv)
            o, cast = _swa_attn(q, kd, vd, b_sinks[j], batch, seq, tq, pending)
            bias_o = b_b_o[j]
        x2 = _proj_ln(o, cast[0], bias_o, x2, mod_l, ln_mix_g[i], ln_mix_b[i], alpha, seq, bm)
        x2 = _mlp_ln(x2, mod_l, cast[1], cast[2], ln_mlp_g[i], ln_mlp_b[i], alpha, seq, bm_mlp, fc)
        if i + 1 < depth:
            w_qkv_bf = cast[3]
    return x2.reshape(batch, seq, d)
```

```python
import functools
import math

import jax
import jax.numpy as jnp
from jax import lax
from jax.experimental import pallas as pl
from jax.experimental.pallas import tpu as pltpu

F32 = jnp.float32
BF16 = jnp.bfloat16

LANES = 128
BF16_ROWS = 16
DA_HEAD_DIM = 128
SW_HEAD_DIM = 64
SW_GROUP = 8
ROW_CHUNK = 256
STRIP = 256
WINDOW = 128
N_MOD = 6
ROPE_THETA = 10000.0
LN_EPS = 1e-5
LOG2E = math.log2(math.e)
VMEM_LIMIT_BYTES = 56 * 1024 * 1024

SH_A, SC_A, G_A, SH_M, SC_M, G_M = range(N_MOD)

_NT = (((1,), (1,)), ((), ()))
_TN = (((0,), (0,)), ((), ()))


def _params(*sem):
    return pltpu.CompilerParams(dimension_semantics=sem, vmem_limit_bytes=VMEM_LIMIT_BYTES)


def _ada_kernel(c_ref, w_ref, b_ref, o_ref):
    c = c_ref[...]
    ca = (c * (1.0 / (1.0 + jnp.exp(-c)))).astype(BF16)
    y = jnp.dot(ca, w_ref[...].astype(BF16), preferred_element_type=F32)
    o_ref[...] = y + b_ref[...]


def _ada_mod(c, w_ada, b_ada):
    depth, d, n = w_ada.shape
    b = c.shape[0]
    rows = 8
    bn = min(n, 1024)
    c8 = jnp.zeros((rows, d), F32).at[:b].set(c)
    out = pl.pallas_call(
        _ada_kernel,
        grid=(depth, n // bn),
        in_specs=[
            pl.BlockSpec((rows, d), lambda l, j: (0, 0)),
            pl.BlockSpec((None, d, bn), lambda l, j: (l, 0, j)),
            pl.BlockSpec((None, 1, bn), lambda l, j: (l, 0, j)),
        ],
        out_specs=pl.BlockSpec((None, rows, bn), lambda l, j: (l, 0, j)),
        out_shape=jax.ShapeDtypeStruct((depth, rows, n), F32),
        compiler_params=_params("parallel", "parallel"),
        name="ada_mod",
    )(c8, w_ada, b_ada.reshape(depth, 1, n))
    return out[:, :b].reshape(depth, b, 1, n)


def _cast_plan(weights, steps, step_of):
    arrays, in_specs, out_specs, out_shapes = [], [], [], []
    for stacked, layer in weights:
        _, rows, cols = stacked.shape
        slab = rows // steps
        assert rows % steps == 0 and slab % BF16_ROWS == 0
        arrays.append(stacked)
        in_specs.append(pl.BlockSpec((None, slab, cols), lambda *g, layer=layer: (layer, step_of(*g), 0)))
        out_specs.append(pl.BlockSpec((slab, cols), lambda *g: (step_of(*g), 0)))
        out_shapes.append(jax.ShapeDtypeStruct((rows, cols), BF16))
    return arrays, in_specs, out_specs, out_shapes


def _cast_slices(src_refs, dst_refs):
    for s, d in zip(src_refs, dst_refs):
        d[...] = s[...].astype(BF16)


def _rope_tables(seq, dim, scale):
    inv = 1.0 / (ROPE_THETA ** (jnp.arange(0, dim, 2, dtype=F32) / dim))
    ang = jnp.arange(seq, dtype=F32)[:, None] * inv[None, :]
    cos, sin = jnp.cos(ang), jnp.sin(ang)
    reps = LANES // dim
    cos_t = jnp.tile(jnp.concatenate([cos, cos], -1), (1, reps))
    sin_t = jnp.tile(jnp.concatenate([-sin, sin], -1), (1, reps))
    return jnp.stack([cos_t * scale, sin_t * scale, cos_t, sin_t])


def _rope(x, cos, sin, half):
    if 2 * half == LANES:
        partner = pltpu.roll(x, half, axis=1)
    else:
        lane = lax.broadcasted_iota(jnp.int32, x.shape, 1)
        lower = (lane % (2 * half)) < half
        partner = jnp.where(lower, pltpu.roll(x, LANES - half, axis=1), pltpu.roll(x, half, axis=1))
    return x * cos + partner * sin


def _qkv_diff_kernel(x_ref, sc_ref, sh_ref, w_ref, tab_ref, o_ref):
    is_v = pl.program_id(0) == 2
    scale = 1.0 + sc_ref[...]
    shift = sh_ref[...]
    w = w_ref[...]
    for r0 in range(0, x_ref.shape[0], ROW_CHUNK):
        rows = slice(r0, r0 + ROW_CHUNK)
        h = (x_ref[rows, :] * scale + shift).astype(BF16)
        y = jnp.dot(h, w, preferred_element_type=F32)
        cos, sin = tab_ref[0, rows, :], tab_ref[1, rows, :]
        for g in range(y.shape[1] // LANES):
            cols = slice(g * LANES, (g + 1) * LANES)
            roped = _rope(y[:, cols], cos, sin, DA_HEAD_DIM // 2)
            o_ref[rows, cols] = jnp.where(is_v, y[:, cols], roped).astype(BF16)


def _qkv_diff(x2, mod_l, w_bf, tabs, seq, bm):
    m, d = x2.shape
    tpb = seq // bm
    return pl.pallas_call(
        _qkv_diff_kernel,
        grid=(3, m // bm),
        in_specs=[
            pl.BlockSpec((bm, d), lambda j, i: (i, 0)),
            pl.BlockSpec((None, 1, d), lambda j, i: (i // tpb, 0, SC_A)),
            pl.BlockSpec((None, 1, d), lambda j, i: (i // tpb, 0, SH_A)),
            pl.BlockSpec((d, d), lambda j, i: (0, j)),
            pl.BlockSpec((2, bm, LANES), lambda j, i: (jnp.minimum(j, 1), i % tpb, 0)),
        ],
        out_specs=pl.BlockSpec((bm, d), lambda j, i: (i, j)),
        out_shape=jax.ShapeDtypeStruct((m, 3 * d), BF16),
        compiler_params=_params("parallel", "parallel"),
        name="qkv_diff",
    )(x2, mod_l, mod_l, w_bf, tabs)


def _diff_attn_kernel(*refs, bq, bk, lam_init, n_cast):
    q_ref, k_ref, v_ref, lq1_ref, lk1_ref, lq2_ref, lk2_ref, g_ref = refs[:8]
    o_ref = refs[8 + n_cast]
    acc_ref, s_ref, p_ref, a_ref = refs[9 + 2 * n_cast:]
    _cast_slices(refs[8:8 + n_cast], refs[9 + n_cast:9 + 2 * n_cast])
    _diff_attn_body(q_ref, k_ref, v_ref, lq1_ref, lk1_ref, lq2_ref, lk2_ref, g_ref, o_ref,
                    acc_ref, s_ref, p_ref, a_ref, bq=bq, bk=bk, lam_init=lam_init)


def _diff_attn_body(q_ref, k_ref, v_ref, lq1_ref, lk1_ref, lq2_ref, lk2_ref, g_ref, o_ref,
                    acc_ref, s_ref, p_ref, a_ref, *, bq, bk, lam_init):
    qi = pl.program_id(2)
    d = DA_HEAD_DIM
    sw = STRIP
    ns = bq // sw
    per = bq // bk
    q = q_ref[...]
    ahead = s_ref.shape[0]
    full = [(t, c) for t in range(2) for c in range(ns)]

    def key_rows(j):
        return pl.ds(pl.multiple_of(j * bk, bk), bk)

    def qk(k, t, c):
        return lax.dot_general(k[:, t * d:(t + 1) * d], q[c * sw:(c + 1) * sw, t * d:(t + 1) * d], _NT,
                               preferred_element_type=F32)

    def pv(v, t, c, p, alpha):
        cols = slice(c * sw, (c + 1) * sw)
        acc_ref[t, :, cols] = alpha * acc_ref[t, :, cols] + lax.dot_general(v, p, _TN, preferred_element_type=F32)

    def block(j, stats, chains, prev_chain, next_chains, diag):
        alpha_prev = a_ref[...]
        k = k_ref[key_rows(j), :]
        v = v_ref[key_rows(j), :]
        n = len(chains)
        s_live, res = {}, {}
        p_prev = None
        for i, (t, c) in enumerate(chains):
            nxt = i + ahead
            if nxt < n:
                s_live[nxt] = qk(k, *chains[nxt])
            elif next_chains is not None:
                s_ref[nxt - n] = qk(k_ref[key_rows(j + 1), :], *next_chains[nxt - n])
            if i == 0:
                pv(v_ref[key_rows(jnp.maximum(j - 1, 0)), :], *prev_chain, p_ref[...], alpha_prev)
            else:
                pv(v, *chains[i - 1], p_prev, alpha_prev)

            s = s_ref[i] if i < ahead else s_live.pop(i)
            if diag is not None and (c + 1) * sw <= (diag + 1) * bk:
                kpos = lax.broadcasted_iota(jnp.int32, s.shape, 0) + diag * bk
                qpos = lax.broadcasted_iota(jnp.int32, s.shape, 1) + c * sw
                s = jnp.where(kpos <= qpos, s, -jnp.inf)
            m_prev = stats[2 * t][:, c * sw:(c + 1) * sw]
            l_prev = stats[2 * t + 1][:, c * sw:(c + 1) * sw]
            m_new = jnp.maximum(m_prev, jnp.max(s, axis=0, keepdims=True))
            alpha_prev = jnp.exp2(m_prev - m_new)
            p = jnp.exp2(s - m_new)
            res[(t, c)] = (m_new, alpha_prev * l_prev + jnp.sum(p, axis=0, keepdims=True))
            p_prev = p.astype(BF16)
        if next_chains is None:
            pv(v, *chains[n - 1], p_prev, alpha_prev)
        else:
            p_ref[...] = p_prev
            a_ref[...] = alpha_prev
        out = []
        for t in range(2):
            for which in range(2):
                out.append(jnp.concatenate(
                    [res[(t, c)][which] if (t, c) in res else stats[2 * t + which][:, c * sw:(c + 1) * sw]
                     for c in range(ns)], axis=1))
        return tuple(out)

    acc_ref[...] = jnp.zeros(acc_ref.shape, F32)
    p_ref[...] = jnp.zeros(p_ref.shape, BF16)
    a_ref[...] = jnp.ones(a_ref.shape, F32)
    k0 = k_ref[key_rows(0), :]
    for i in range(ahead):
        s_ref[i] = qk(k0, *full[i])
    neg = jnp.full((1, bq), -jnp.inf, F32)
    zero = jnp.zeros((1, bq), F32)
    stats = lax.fori_loop(0, per * qi, lambda j, st: block(j, st, full, full[-1], full, None),
                          (neg, zero, neg, zero))
    prev = full[-1]
    for dg in range(per):
        chains = [(t, c) for (t, c) in full if (c + 1) * sw > dg * bk]
        nxt = None if dg == per - 1 else [(t, c) for (t, c) in full if (c + 1) * sw > (dg + 1) * bk]
        stats = block(per * qi + dg, stats, chains, prev, nxt, dg)
        prev = chains[-1]
    _, l1, _, l2 = stats

    lam = (jnp.exp(jnp.sum(lq1_ref[...] * lk1_ref[...], keepdims=True))
           - jnp.exp(jnp.sum(lq2_ref[...] * lk2_ref[...], keepdims=True)) + lam_init)
    of = acc_ref[0] / l1 - lam * (acc_ref[1] / l2)
    of = of * lax.rsqrt(jnp.mean(of * of, axis=0, keepdims=True) + LN_EPS)
    o_ref[...] = (of.T * g_ref[...] * (1.0 - lam_init)).astype(BF16)


def _diff_attn(qkv, lam_vecs, subln_g, layer_idx, batch, seq, bq, bk, cast_weights):
    m, d3 = qkv.shape
    d = d3 // 3
    w = 2 * DA_HEAD_DIM
    heads = d // w
    nq = seq // bq
    assert bq % bk == 0 and bk % STRIP == 0
    lam_init = 0.8 - 0.6 * math.exp(-0.3 * layer_idx)
    vec = pl.BlockSpec((1, DA_HEAD_DIM), lambda b, h, i: (0, 0))
    c_arrays, c_in, c_out, c_shapes = _cast_plan(cast_weights, batch * heads * nq,
                                                 lambda b, h, i: (b * heads + h) * nq + i)
    outs = pl.pallas_call(
        functools.partial(_diff_attn_kernel, bq=bq, bk=bk, lam_init=lam_init, n_cast=len(c_arrays)),
        grid=(batch, heads, nq),
        in_specs=[
            pl.BlockSpec((bq, w), lambda b, h, i: (b * nq + i, h)),
            pl.BlockSpec((seq, w), lambda b, h, i: (b, heads + h)),
            pl.BlockSpec((seq, w), lambda b, h, i: (b, 2 * heads + h)),
            vec, vec, vec, vec,
            pl.BlockSpec((1, w), lambda b, h, i: (0, 0)),
            *c_in,
        ],
        out_specs=[pl.BlockSpec((bq, w), lambda b, h, i: (b * nq + i, h)), *c_out],
        out_shape=[jax.ShapeDtypeStruct((m, d), BF16), *c_shapes],
        scratch_shapes=[
            pltpu.VMEM((2, w, bq), F32),
            pltpu.VMEM((2, bk, STRIP), F32),
            pltpu.VMEM((bk, STRIP), BF16),
            pltpu.VMEM((1, STRIP), F32),
        ],
        compiler_params=_params("parallel", "parallel", "arbitrary"),
        name="diff_attn",
    )(qkv, qkv, qkv, *[v.reshape(1, DA_HEAD_DIM) for v in lam_vecs], subln_g.reshape(1, w), *c_arrays)
    return outs[0], outs[1:]


def _qkv_swa_kernel(x_ref, sc_ref, sh_ref, w_ref, b_ref, tab_ref, q_ref, k_ref, v_ref):
    half = SW_HEAD_DIM // 2
    dq = q_ref.shape[1]
    dkv = k_ref.shape[1] // 2
    scale = 1.0 + sc_ref[...]
    shift = sh_ref[...]
    w = w_ref[...]
    lower = lax.broadcasted_iota(jnp.int32, (ROW_CHUNK, LANES), 1) < SW_HEAD_DIM

    for r0 in range(0, x_ref.shape[0], ROW_CHUNK):
        rows = slice(r0, r0 + ROW_CHUNK)
        h = (x_ref[rows, :] * scale + shift).astype(BF16)
        y = jnp.dot(h, w, preferred_element_type=F32) + b_ref[...]
        cq, sq, ck, sk = (tab_ref[t, rows, :] for t in range(4))
        for g in range(dq // LANES):
            cols = slice(g * LANES, (g + 1) * LANES)
            q_ref[rows, cols] = _rope(y[:, cols], cq, sq, half).astype(BF16)

        def spread(t, ref, p):
            swapped = pltpu.roll(t, SW_HEAD_DIM, axis=1)
            ref[rows, (2 * p) * LANES:(2 * p + 1) * LANES] = jnp.where(lower, t, swapped).astype(BF16)
            ref[rows, (2 * p + 1) * LANES:(2 * p + 2) * LANES] = jnp.where(lower, swapped, t).astype(BF16)

        for p in range(dkv // LANES):
            kc = slice(dq + p * LANES, dq + (p + 1) * LANES)
            vc = slice(dq + dkv + p * LANES, dq + dkv + (p + 1) * LANES)
            spread(_rope(y[:, kc], ck, sk, half), k_ref, p)
            spread(y[:, vc], v_ref, p)


def _qkv_swa(x2, mod_l, w_bf, bias, tabs, seq, bm):
    m, d = x2.shape
    n = w_bf.shape[1]
    dkv = (n - d) // 2
    assert dkv % LANES == 0
    tpb = seq // bm
    kv_shape = jax.ShapeDtypeStruct((m, 2 * dkv), BF16)
    return pl.pallas_call(
        _qkv_swa_kernel,
        grid=(m // bm,),
        in_specs=[
            pl.BlockSpec((bm, d), lambda i: (i, 0)),
            pl.BlockSpec((None, 1, d), lambda i: (i // tpb, 0, SC_A)),
            pl.BlockSpec((None, 1, d), lambda i: (i // tpb, 0, SH_A)),
            pl.BlockSpec((d, n), lambda i: (0, 0)),
            pl.BlockSpec((1, n), lambda i: (0, 0)),
            pl.BlockSpec((4, bm, LANES), lambda i: (0, i % tpb, 0)),
        ],
        out_specs=[
            pl.BlockSpec((bm, d), lambda i: (i, 0)),
            pl.BlockSpec((bm, 2 * dkv), lambda i: (i, 0)),
            pl.BlockSpec((bm, 2 * dkv), lambda i: (i, 0)),
        ],
        out_shape=[jax.ShapeDtypeStruct((m, d), BF16), kv_shape, kv_shape],
        compiler_params=_params("parallel"),
        name="qkv_swa",
    )(x2, mod_l, mod_l, w_bf, bias.reshape(1, n), tabs)


def _swa_kernel(*refs, tq, n_cast):
    sink_ref, q_ref, kc_ref, kp_ref, vc_ref, vp_ref = refs[:6]
    o_ref = refs[6 + n_cast]
    _cast_slices(refs[6:6 + n_cast], refs[7 + n_cast:])
    n = pl.program_id(1)
    blk = WINDOW
    pairs = SW_GROUP // 2
    kvh_count = kc_ref.shape[1] // LANES
    cols8 = SW_GROUP * blk

    c = lax.broadcasted_iota(jnp.int32, (2 * blk, cols8), 0)
    r = lax.broadcasted_iota(jnp.int32, (2 * blk, cols8), 1) % blk
    delta = c - r - 1
    neg = jnp.float32(-jnp.inf)
    bias_rest = jnp.where((delta >= 0) & (delta < blk), 0.0, neg)
    lo_bound = jnp.where(n == 0, blk, 0)
    bias_first = jnp.where(c >= lo_bound, bias_rest, neg)

    lane = lax.broadcasted_iota(jnp.int32, (1, LANES), 1)
    m_lo = jnp.where(lane < SW_HEAD_DIM, 1.0, 0.0).astype(BF16)
    m_hi = jnp.where(lane < SW_HEAD_DIM, 0.0, 1.0).astype(BF16)
    upper = lax.broadcasted_iota(jnp.int32, (LANES, blk), 0) < SW_HEAD_DIM

    units = [(i, kvh) for i in range(tq // blk) for kvh in range(kvh_count)]

    def scores(i, kvh):
        qrows = slice(i * blk, (i + 1) * blk)
        kcols = slice(kvh * LANES, (kvh + 1) * LANES)
        if i == 0:
            kwin = jnp.concatenate([kp_ref[:, kcols], kc_ref[0:blk, kcols]], axis=0)
        else:
            kwin = kc_ref[(i - 1) * blk:(i + 1) * blk, kcols]
        parts = []
        for j in range(pairs):
            col0 = (kvh * pairs + j) * LANES
            qj = q_ref[qrows, col0:col0 + LANES]
            parts += [qj * m_lo, qj * m_hi]
        q8 = jnp.concatenate(parts, axis=0)
        return lax.dot_general(kwin, q8, _NT, preferred_element_type=F32)

    def softmax(s, i, kvh):
        sinks = [jnp.full((1, blk), sink_ref[kvh * SW_GROUP + g] * LOG2E, F32) for g in range(SW_GROUP)]
        sink = jnp.concatenate(sinks, axis=1)
        s = s + (bias_first if i == 0 else bias_rest)
        mx = jnp.maximum(jnp.max(s, axis=0, keepdims=True), sink)
        e = jnp.exp2(s - mx)
        denom = jnp.sum(e, axis=0, keepdims=True) + jnp.exp2(sink - mx)
        return e.astype(BF16), denom

    def output(e, denom, i, kvh):
        qrows = slice(i * blk, (i + 1) * blk)
        kcols = slice(kvh * LANES, (kvh + 1) * LANES)
        if i == 0:
            vwin = jnp.concatenate([vp_ref[:, kcols], vc_ref[0:blk, kcols]], axis=0)
        else:
            vwin = vc_ref[(i - 1) * blk:(i + 1) * blk, kcols]
        o = lax.dot_general(vwin, e, _TN, preferred_element_type=F32) / denom
        for j in range(pairs):
            even = o[:, (2 * j) * blk:(2 * j + 1) * blk]
            odd = o[:, (2 * j + 1) * blk:(2 * j + 2) * blk]
            col0 = (kvh * pairs + j) * LANES
            o_ref[qrows, col0:col0 + LANES] = jnp.where(upper, even, odd).T.astype(BF16)

    s_next = scores(*units[0])
    pending = None
    for u, unit in enumerate(units):
        s_cur = s_next
        if u + 1 < len(units):
            s_next = scores(*units[u + 1])
        if pending is not None:
            output(*pending)
        pending = softmax(s_cur, *unit) + unit
    output(*pending)


def _swa_attn(q, kd, vd, sinks, batch, seq, tq, cast_weights):
    m, d = q.shape
    dk = kd.shape[1]
    nt = seq // tq
    per = tq // WINDOW

    def prev_map(b, n):
        return (jnp.maximum(b * (seq // WINDOW) + n * per - 1, 0), 0)

    cur = pl.BlockSpec((tq, dk), lambda b, n: (b * nt + n, 0))
    prev = pl.BlockSpec((WINDOW, dk), prev_map)
    c_arrays, c_in, c_out, c_shapes = _cast_plan(cast_weights, batch * nt, lambda b, n: b * nt + n)
    outs = pl.pallas_call(
        functools.partial(_swa_kernel, tq=tq, n_cast=len(c_arrays)),
        grid=(batch, nt),
        in_specs=[
            pl.BlockSpec(memory_space=pltpu.SMEM),
            pl.BlockSpec((tq, d), lambda b, n: (b * nt + n, 0)),
            cur, prev, cur, prev,
            *c_in,
        ],
        out_specs=[pl.BlockSpec((tq, d), lambda b, n: (b * nt + n, 0)), *c_out],
        out_shape=[jax.ShapeDtypeStruct((m, d), BF16), *c_shapes],
        compiler_params=_params("parallel", "parallel"),
        name="swa_attn",
    )(sinks, q, kd, kd, vd, vd, *c_arrays)
    return outs[0], outs[1:]


def _layer_norm(z, g, b):
    mu = jnp.mean(z, axis=-1, keepdims=True)
    zc = z - mu
    var = jnp.mean(zc * zc, axis=-1, keepdims=True)
    return zc * lax.rsqrt(var + LN_EPS) * g + b


def _proj_ln_kernel(o_ref, w_ref, b_ref, x_ref, gate_ref, g_ref, beta_ref, out_ref, *, alpha):
    w = w_ref[...]
    gate = 1.0 + gate_ref[...]
    for r0 in range(0, o_ref.shape[0], ROW_CHUNK):
        rows = slice(r0, r0 + ROW_CHUNK)
        y = jnp.dot(o_ref[rows, :], w, preferred_element_type=F32) + b_ref[...]
        z = alpha * x_ref[rows, :] + gate * y
        out_ref[rows, :] = _layer_norm(z, g_ref[...], beta_ref[...])


def _proj_ln(o, w_bf, bias, x2, mod_l, ln_g, ln_b, alpha, seq, bm):
    m, d = x2.shape
    tpb = seq // bm
    row = pl.BlockSpec((1, d), lambda i: (0, 0))
    return pl.pallas_call(
        functools.partial(_proj_ln_kernel, alpha=alpha),
        grid=(m // bm,),
        in_specs=[
            pl.BlockSpec((bm, d), lambda i: (i, 0)),
            pl.BlockSpec((d, d), lambda i: (0, 0), pipeline_mode=pl.Buffered(1)),
            row,
            pl.BlockSpec((bm, d), lambda i: (i, 0)),
            pl.BlockSpec((None, 1, d), lambda i: (i // tpb, 0, G_A)),
            row, row,
        ],
        out_specs=pl.BlockSpec((bm, d), lambda i: (i, 0)),
        out_shape=jax.ShapeDtypeStruct((m, d), F32),
        compiler_params=_params("parallel"),
        name="proj_ln",
    )(o, w_bf, bias.reshape(1, d), x2, mod_l, ln_g.reshape(1, d), ln_b.reshape(1, d))


def _mlp_kernel(x_ref, sc_ref, sh_ref, gate_ref, wu_ref, wd_ref, g_ref, beta_ref, out_ref, *, alpha):
    k = pl.program_id(1)
    last = pl.num_programs(1) - 1
    scale = 1.0 + sc_ref[...]
    shift = sh_ref[...]

    def hidden_chunk(rows):
        h = (x_ref[rows, :] * scale + shift).astype(BF16)
        u = jnp.maximum(jnp.dot(h, wu_ref[...], preferred_element_type=F32), 0.0)
        return jnp.dot((u * u).astype(BF16), wd_ref[...], preferred_element_type=F32)

    everything = slice(0, x_ref.shape[0])

    @pl.when(k == 0)
    def _():
        out_ref[...] = hidden_chunk(everything)

    @pl.when((k > 0) & (k < last))
    def _():
        out_ref[...] += hidden_chunk(everything)

    @pl.when(k == last)
    def _():
        gate = 1.0 + gate_ref[...]
        for r0 in range(0, x_ref.shape[0], ROW_CHUNK):
            rows = slice(r0, r0 + ROW_CHUNK)
            y = out_ref[rows, :] + hidden_chunk(rows)
            z = alpha * x_ref[rows, :] + gate * y
            out_ref[rows, :] = _layer_norm(z, g_ref[...], beta_ref[...])


def _mlp_ln(x2, mod_l, wu_bf, wd_bf, ln_g, ln_b, alpha, seq, bm, fc):
    m, d = x2.shape
    f = wu_bf.shape[1]
    tpb = seq // bm
    row = pl.BlockSpec((1, d), lambda i, k: (0, 0))

    def mod(which):
        return pl.BlockSpec((None, 1, d), lambda i, k: (i // tpb, 0, which))

    return pl.pallas_call(
        functools.partial(_mlp_kernel, alpha=alpha),
        grid=(m // bm, f // fc),
        in_specs=[
            pl.BlockSpec((bm, d), lambda i, k: (i, 0)),
            mod(SC_M), mod(SH_M), mod(G_M),
            pl.BlockSpec((d, fc), lambda i, k: (0, k)),
            pl.BlockSpec((fc, d), lambda i, k: (k, 0)),
            row, row,
        ],
        out_specs=pl.BlockSpec((bm, d), lambda i, k: (i, 0)),
        out_shape=jax.ShapeDtypeStruct((m, d), F32),
        compiler_params=_params("parallel", "arbitrary"),
        name="mlp_ln",
    )(x2, mod_l, mod_l, mod_l, wu_bf, wd_bf, ln_g.reshape(1, d), ln_b.reshape(1, d))


def kernel(x, c, w_ada, b_ada, ln_mix_g, ln_mix_b, ln_mlp_g, ln_mlp_b, w_up, w_down, a_w_qkv, a_w_o, a_lambda_q1, a_lambda_k1, a_lambda_q2, a_lambda_k2, a_subln_g, b_w_qkv, b_b_qkv, b_sinks, b_w_o, b_b_o):
    batch, seq, d = x.shape
    depth = w_ada.shape[0]
    alpha = (2 * depth) ** 0.25
    m = batch * seq
    bm = min(512, seq)
    bm_qkv = min(1024, seq)
    bm_mlp = min(1024, seq)
    fc = min(1024, w_up.shape[-1])
    bq_att = min(2048, seq)
    bk_att = min(512, seq)
    tq = min(512, seq)

    mod = _ada_mod(c, w_ada, b_ada)
    tabs_a = _rope_tables(seq, DA_HEAD_DIM, DA_HEAD_DIM ** -0.5 * LOG2E)
    tabs_b = _rope_tables(seq, SW_HEAD_DIM, SW_HEAD_DIM ** -0.5 * LOG2E)
    zero_bias = jnp.zeros((d,), F32)

    x2 = x.reshape(m, d)
    w_qkv_bf = a_w_qkv[0].astype(BF16)
    for i in range(depth):
        j = i // 2
        mod_l = mod[i]
        is_a = i % 2 == 0
        pending = [(a_w_o if is_a else b_w_o, j), (w_up, i), (w_down, i)]
        if i + 1 < depth:
            pending.append((b_w_qkv if is_a else a_w_qkv, (i + 1) // 2))
        if is_a:
            qkv = _qkv_diff(x2, mod_l, w_qkv_bf, tabs_a, seq, bm_qkv)
            lam_vecs = (a_lambda_q1[j], a_lambda_k1[j], a_lambda_q2[j], a_lambda_k2[j])
            o, cast = _diff_attn(qkv, lam_vecs, a_subln_g[j], i, batch, seq, bq_att, bk_att, pending)
            bias_o = zero_bias
        else:
            q, kd, vd = _qkv_swa(x2, mod_l, w_qkv_bf, b_b_qkv[j], tabs_b, seq, bm_qkv)
            o, cast = _swa_attn(q, kd, vd, b_sinks[j], batch, seq, tq, pending)
            bias_o = b_b_o[j]
        x2 = _proj_ln(o, cast[0], bias_o, x2, mod_l, ln_mix_g[i], ln_mix_b[i], alpha, seq, bm)
        x2 = _mlp_ln(x2, mod_l, cast[1], cast[2], ln_mlp_g[i], ln_mlp_b[i], alpha, seq, bm_mlp, fc)
        if i + 1 < depth:
            w_qkv_bf = cast[3]
    return x2.reshape(batch, seq, d)
```
